```python
import jax, jax.numpy as jnp
from jax import lax
import numpy as np

D_MODEL = 1024
BATCH = 8
SEQ = 4096
DEPTH = 2

D_MIX = 512
N_BRANCH = 3
GLA_HEADS = 4
GLA_DK = 64
GLA_DV = 128
GLA_RANK = 16
GLA_TAU = 16.0
GLA_CHUNK = 64
POOL_WINDOWS = (2, 4, 8, 16)
POOL_GROUPS = 4
POOL_GC = D_MIX // POOL_GROUPS
MOBA_HEADS = 8
MOBA_DH = D_MIX // MOBA_HEADS
MOBA_BLOCK = 256
MOBA_TOPK = 3
MOBA_QBLOCK = 128
D_FF = -(-8 * D_MODEL // (3 * 256)) * 256
EPS = 1e-6
NEG = -1e30

IN_SIZES = (GLA_HEADS * GLA_DK, GLA_HEADS * GLA_DK, GLA_HEADS * GLA_DV, GLA_RANK,
            GLA_HEADS * GLA_DV, D_MIX, D_MIX, D_MIX, D_MIX, N_BRANCH * D_MODEL)
D_IN = sum(IN_SIZES)

kernel_name = "hybrid_gla_pool_moba_block"


def rmsnorm(x, w):
    xf = x.astype(jnp.float32)
    y = xf * lax.rsqrt(jnp.mean(xf * xf, axis=-1, keepdims=True) + EPS)
    return (y * w.astype(jnp.float32)).astype(x.dtype)


def gla_mixer(q, k, v, g1, r, w_g2, b_g, norm_w):
    B, T, _ = q.shape
    H, DK, DV, C = GLA_HEADS, GLA_DK, GLA_DV, GLA_CHUNK
    NC = T // C
    f32 = jnp.float32
    qf = q.astype(f32).reshape(B, T, H, DK) * (DK ** -0.5)
    kf = k.astype(f32).reshape(B, T, H, DK)
    vf = v.astype(f32).reshape(B, T, H, DV)
    logit = (g1 @ w_g2 + b_g).astype(f32)
    log_a = (jax.nn.log_sigmoid(logit) / GLA_TAU).reshape(B, T, H, DK)

    def to_chunks(a):
        return a.reshape(B, NC, C, H, a.shape[-1]).transpose(1, 0, 3, 2, 4)

    causal = jnp.tril(jnp.ones((C, C), dtype=bool))

    def step(S, inp):
        qc, kc, vc, gc = inp
        bc = jnp.cumsum(gc, axis=2)
        diff = bc[:, :, :, None, :] - bc[:, :, None, :, :]
        decay = jnp.exp(jnp.where(causal[None, None, :, :, None], diff, NEG))
        attn = jnp.einsum('bhtd,bhsd,bhtsd->bhts', qc, kc, decay)
        o = (jnp.einsum('bhts,bhsv->bhtv', attn, vc)
             + jnp.einsum('bhtd,bhdv->bhtv', qc * jnp.exp(bc), S))
        b_last = bc[:, :, -1:, :]
        S = (jnp.exp(b_last[:, :, 0, :])[..., None] * S
             + jnp.einsum('bhsd,bhsv->bhdv', kc * jnp.exp(b_last - bc), vc))
        return S, o

    S0 = jnp.zeros((B, H, DK, DV), f32)
    _, o = lax.scan(step, S0, (to_chunks(qf), to_chunks(kf), to_chunks(vf), to_chunks(log_a)))
    o = o.transpose(1, 0, 3, 2, 4).reshape(B, T, H, DV)
    o = o * lax.rsqrt(jnp.mean(o * o, axis=-1, keepdims=True) + EPS) * norm_w.astype(f32)
    o = o.reshape(B, T, H * DV) * jax.nn.silu(r.astype(f32))
    return o.astype(q.dtype)


def pool_mixer(u, w_pool, scale):
    B, T, _ = u.shape
    f32 = jnp.float32
    uf = u.astype(f32).reshape(B, T, POOL_GROUPS, POOL_GC)
    cs = jnp.cumsum(uf, axis=1)
    t = jnp.arange(T)
    outs = []
    for gi, w in enumerate(POOL_WINDOWS):
        c = cs[:, :, gi]
        lag = jnp.pad(c, ((0, 0), (w, 0), (0, 0)))[:, :T]
        cnt = jnp.minimum(t + 1, w).astype(f32)[None, :, None]
        outs.append((c - lag) / cnt - uf[:, :, gi])
    p = jnp.stack(outs, axis=2)
    y = jnp.einsum('btgc,gcd->btgd', p, w_pool.astype(f32)).reshape(B, T, D_MIX)
    return (y * scale.astype(f32)).astype(u.dtype)


def moba_mixer(q, k, v):
    B, T, _ = q.shape
    H, DH, BLK, QB = MOBA_HEADS, MOBA_DH, MOBA_BLOCK, MOBA_QBLOCK
    NB = -(-T // BLK)
    NQ = T // QB
    KSEL = min(MOBA_TOPK, NB)
    f32 = jnp.float32
    qh = q.reshape(B, T, H, DH).transpose(0, 2, 1, 3)
    kh = k.reshape(B, T, H, DH).transpose(0, 2, 1, 3)
    vh = v.reshape(B, T, H, DH).transpose(0, 2, 1, 3)
    pad = NB * BLK - T
    kp = jnp.pad(kh, ((0, 0), (0, 0), (0, pad), (0, 0)))
    vp = jnp.pad(vh, ((0, 0), (0, 0), (0, pad), (0, 0)))
    kblk = kp.reshape(B, H, NB, BLK, DH)
    vblk = vp.reshape(B, H, NB, BLK, DH)
    kmean = jnp.mean(kblk.astype(f32), axis=3)
    gate = jnp.einsum('bhtd,bhnd->bhtn', qh.astype(f32), kmean)
    tblk = jnp.arange(T) // BLK
    past = jnp.arange(NB)[None, :] < tblk[:, None]
    gate = jnp.where(past[None, None], gate, NEG)
    _, idx = lax.top_k(gate, KSEL)
    valid = jnp.arange(KSEL)[None, :] < tblk[:, None]
    slopes = (2.0 ** (-8.0 * (jnp.arange(H) + 1) / H)).astype(f32)
    scale = DH ** -0.5
    hi = jnp.arange(H)[:, None, None]

    def one_block(i):
        b = i // NQ
        t0 = (i % NQ) * QB
        qb = lax.dynamic_slice(qh, (b, 0, t0, 0), (1, H, QB, DH))[0]
        ib = lax.dynamic_slice(idx, (b, 0, t0, 0), (1, H, QB, KSEL))[0]
        vm = lax.dynamic_slice(valid, (t0, 0), (QB, KSEL))
        kg = kblk[b][hi, ib]
        vg = vblk[b][hi, ib]
        own0 = (t0 // BLK) * BLK
        ko = lax.dynamic_slice(kp, (b, 0, own0, 0), (1, H, BLK, DH))[0]
        vo = lax.dynamic_slice(vp, (b, 0, own0, 0), (1, H, BLK, DH))[0]
        tq = t0 + jnp.arange(QB)
        s_sel = ib[..., None] * BLK + jnp.arange(BLK)
        s_own = own0 + jnp.arange(BLK)
        sc_sel = (jnp.einsum('hqd,hqkjd->hqkj', qb, kg).astype(f32) * scale
                  - slopes[:, None, None, None] * (tq[None, :, None, None] - s_sel).astype(f32))
        sc_sel = jnp.where(vm[None, :, :, None], sc_sel, NEG)
        sc_own = (jnp.einsum('hqd,hjd->hqj', qb, ko).astype(f32) * scale
                  - slopes[:, None, None] * (tq[:, None] - s_own[None, :]).astype(f32)[None])
        sc_own = jnp.where((s_own[None, :] <= tq[:, None])[None], sc_own, NEG)
        sc = jnp.concatenate([sc_sel.reshape(H, QB, KSEL * BLK), sc_own], axis=-1)
        p = jax.nn.softmax(sc, axis=-1).astype(v.dtype)
        p_sel = p[..., :KSEL * BLK].reshape(H, QB, KSEL, BLK)
        p_own = p[..., KSEL * BLK:]
        return (jnp.einsum('hqkj,hqkjd->hqd', p_sel, vg)
                + jnp.einsum('hqj,hjd->hqd', p_own, vo))

    out = lax.map(one_block, jnp.arange(B * NQ))
    return out.reshape(B, NQ, H, QB, DH).transpose(0, 1, 3, 2, 4).reshape(B, T, H * DH)


def setup_inputs(seed: int = 0) -> dict:
    key = jax.random.key(seed)
    ks = jax.random.split(key, 20)
    f32 = jnp.float32

    def nrm(k, shape, fan_in):
        return jax.random.normal(k, shape, f32) * (fan_in ** -0.5)

    def gain(k, shape):
        return 1.0 + 0.05 * jax.random.normal(k, shape, f32)

    L = DEPTH
    return {
        "x": jax.random.normal(ks[0], (BATCH, SEQ, D_MODEL), f32),
        "norm_mix_pre": gain(ks[1], (L, D_MODEL)),
        "w_in": nrm(ks[2], (L, D_MODEL, D_IN), D_MODEL),
        "gla_w_g2": nrm(ks[3], (L, GLA_RANK, GLA_HEADS * GLA_DK), GLA_RANK),
        "gla_b_g": 0.1 * jax.random.normal(ks[4], (L, GLA_HEADS * GLA_DK), f32),
        "gla_norm": gain(ks[5], (L, GLA_DV)),
        "pool_w": nrm(ks[6], (L, POOL_GROUPS, POOL_GC, POOL_GC), POOL_GC),
        "pool_scale": gain(ks[7], (L, D_MIX)),
        "w_branch_a": nrm(ks[8], (L, D_MIX, D_MODEL), D_MIX),
        "w_branch_b": nrm(ks[9], (L, D_MIX, D_MODEL), D_MIX),
        "w_branch_c": nrm(ks[10], (L, D_MIX, D_MODEL), D_MIX),
        "w_out": nrm(ks[11], (L, D_MODEL, D_MODEL), D_MODEL),
        "norm_mix_post": gain(ks[12], (L, D_MODEL)),
        "norm_ffn_pre": gain(ks[13], (L, D_MODEL)),
        "ffn_w_gate": nrm(ks[14], (L, D_MODEL, D_FF), D_MODEL),
        "ffn_w_up": nrm(ks[15], (L, D_MODEL, D_FF), D_MODEL),
        "ffn_w_down": nrm(ks[16], (L, D_FF, D_MODEL), D_FF),
        "norm_ffn_post": gain(ks[17], (L, D_MODEL)),
    }


def reference(x, norm_mix_pre, w_in, gla_w_g2, gla_b_g, gla_norm, pool_w, pool_scale,
              w_branch_a, w_branch_b, w_branch_c, w_out, norm_mix_post, norm_ffn_pre,
              ffn_w_gate, ffn_w_up, ffn_w_down, norm_ffn_post):
    B, T, D = x.shape
    splits = [int(s) for s in np.cumsum(IN_SIZES)[:-1]]
    for l in range(DEPTH):
        h = rmsnorm(x, norm_mix_pre[l])
        proj = h @ w_in[l]
        gq, gk, gv, gg1, gr, pu, mq, mk, mv, gates = jnp.split(proj, splits, axis=-1)
        ya = gla_mixer(gq, gk, gv, gg1, gr, gla_w_g2[l], gla_b_g[l], gla_norm[l]) @ w_branch_a[l]
        yb = pool_mixer(pu, pool_w[l], pool_scale[l]) @ w_branch_b[l]
        yc = moba_mixer(mq, mk, mv) @ w_branch_c[l]
        gs = jax.nn.sigmoid(gates).reshape(B, T, N_BRANCH, D)
        mixed = gs[:, :, 0] * ya + gs[:, :, 1] * yb + gs[:, :, 2] * yc
        x = x + rmsnorm(mixed @ w_out[l], norm_mix_post[l])
        h = rmsnorm(x, norm_ffn_pre[l])
        f = (jax.nn.silu(h @ ffn_w_gate[l]) * (h @ ffn_w_up[l])) @ ffn_w_down[l]
        x = x + rmsnorm(f, norm_ffn_post[l])
    return x
```

```python
import functools

import numpy as np
import jax
import jax.numpy as jnp
from jax import lax
from jax.experimental import pallas as pl
from jax.experimental.pallas import tpu as pltpu

F32 = jnp.float32
BF16 = jnp.bfloat16

D_MODEL = 1024
D_MIX = 512
GLA_HEADS = 4
GLA_DK = 64
GLA_DV = 128
GLA_RANK = 16
GLA_TAU = 16.0
GLA_CHUNK = 64
POOL_WINDOWS = (2, 4, 8, 16)
POOL_GC = 128
MOBA_HEADS = 8
MOBA_DH = 64
MOBA_BLOCK = 256
MOBA_TOPK = 3
D_FF = 2816
EPS = 1e-6
NEG = -1e30

LANES = 128
VMEM_LIMIT = 56 * 1024 * 1024

_IN_WIDTHS = (512, 512, 512, 512, 512, 512, 512, 3072, LANES)
_ROW_TILE = 512


def _dot(a, b):
    return jnp.dot(a, b, preferred_element_type=F32)


def _dot_nt(a, b):
    return lax.dot_general(a, b, (((1,), (1,)), ((), ())), preferred_element_type=F32)


def _dot_tn(a, b):
    return lax.dot_general(a, b, (((0,), (0,)), ((), ())), preferred_element_type=F32)


def _rms(x, w):
    ms = jnp.mean(x * x, axis=-1, keepdims=True)
    return x * lax.rsqrt(ms + EPS) * w


def _sigmoid(x):
    return 1.0 / (1.0 + jnp.exp(-x))


def _params(*sem):
    return pltpu.CompilerParams(dimension_semantics=sem, vmem_limit_bytes=VMEM_LIMIT)


def _const_spec(shape):
    nd = len(shape)
    return pl.BlockSpec(shape, lambda *_: (0,) * nd, pipeline_mode=pl.Buffered(1))


def _inproj_kernel(x_ref, nw_ref, w_ref, *o_refs):
    h = _rms(x_ref[...], nw_ref[...]).astype(BF16)
    off = 0
    for o_ref in o_refs:
        width = o_ref.shape[-1]
        for c0 in range(0, width, 512):
            cw = min(512, width - c0)
            acc = _dot(h, w_ref[:, off + c0:off + c0 + cw])
            o_ref[:, c0:c0 + cw] = acc.astype(o_ref.dtype)
        off += width


def _inproj(x, nw, wcat):
    n = x.shape[0]
    tm = _ROW_TILE
    total = sum(_IN_WIDTHS)
    return pl.pallas_call(
        _inproj_kernel,
        grid=(n // tm,),
        in_specs=[pl.BlockSpec((tm, D_MODEL), lambda i: (i, 0)),
                  _const_spec((1, D_MODEL)),
                  _const_spec((D_MODEL, total))],
        out_specs=[pl.BlockSpec((tm, w), lambda i: (i, 0)) for w in _IN_WIDTHS],
        out_shape=[jax.ShapeDtypeStruct((n, w), BF16) for w in _IN_WIDTHS],
        compiler_params=_params("arbitrary"),
        name="inproj",
    )(x, nw.reshape(1, D_MODEL), wcat)


_GLA_HALVES = (32, 16, 8, 4, 2, 1)


def _gla_consts():
    c = GLA_CHUNK
    t = np.arange(c)
    tri = (t[None, :] <= t[:, None]).astype(np.float32)
    blocks = [tri]
    masks = []
    for m in _GLA_HALVES:
        ref = (t // (2 * m)) * (2 * m) + m - 1
        blocks.append(tri[ref])
        same = (t[:, None] // (2 * m)) == (t[None, :] // (2 * m))
        upper = (t[:, None] % (2 * m)) >= m
        lower = (t[None, :] % (2 * m)) < m
        masks.append((same & upper & lower).astype(np.float32))
    blocks.append(tri[np.full(c, c - 1)])
    masks.append(np.eye(c, dtype=np.float32))
    sel = np.concatenate(blocks, 0)
    return np.concatenate([sel, sel], 1), np.stack(masks)


def _gla_kernel(qk_ref, v_ref, g1_ref, r_ref, wg2_ref, bg_ref, nw_ref, sel_ref, mask_ref,
                o_ref, st_ref, *, n_chunks):
    c = GLA_CHUNK
    nlev = len(_GLA_HALVES)

    @pl.when(pl.program_id(1) == 0)
    def _():
        st_ref[...] = jnp.zeros_like(st_ref)

    lane = lax.broadcasted_iota(jnp.int32, (c, LANES), 1)
    first_half = lane < GLA_DK
    lane_st = lax.broadcasted_iota(jnp.int32, (GLA_DV, LANES), 1) < GLA_DK

    def chunk(ci, carry):
        r0 = pl.multiple_of(ci * c, c)
        qk = qk_ref[0, pl.ds(r0, c), :]
        q = qk[:, :256].astype(F32) * (GLA_DK ** -0.5)
        k = qk[:, 256:].astype(F32)
        v = v_ref[0, pl.ds(r0, c), :]
        logit = _dot(g1_ref[0, pl.ds(r0, c), :], wg2_ref[...]) + bg_ref[...]
        log_a = (jnp.minimum(logit, 0.0) - jnp.log(1.0 + jnp.exp(-jnp.abs(logit)))) * (1.0 / GLA_TAU)
        la_hi = log_a.astype(BF16)
        la_lo = (log_a - la_hi.astype(F32)).astype(BF16)
        rb = _dot(sel_ref[...], jnp.concatenate([la_hi, la_lo], axis=0))
        bc = rb[0:c]
        b_last = rb[(nlev + 1) * c:(nlev + 2) * c]
        q_dec = (q * jnp.exp(bc)).astype(BF16)
        k_dec = (k * jnp.exp(b_last - bc)).astype(BF16)
        st_decay = jnp.exp(b_last[0:1, :])

        a = [jnp.zeros((c, c), F32) for _ in range(GLA_HEADS)]
        for lv in range(nlev + 1):
            if lv < nlev:
                w = jnp.exp(-jnp.abs(bc - rb[(lv + 1) * c:(lv + 2) * c]))
                qs = (q * w).astype(BF16)
                ks = (k * w).astype(BF16)
            else:
                qs = q.astype(BF16)
                ks = k.astype(BF16)
            msk = mask_ref[lv]
            for h in range(GLA_HEADS):
                pr = h // 2
                qp = qs[:, pr * LANES:(pr + 1) * LANES]
                kp = ks[:, pr * LANES:(pr + 1) * LANES]
                keep = first_half if h % 2 == 0 else jnp.logical_not(first_half)
                qm = jnp.where(keep, qp, jnp.zeros_like(qp))
                a[h] = a[h] + _dot_nt(qm, kp) * msk

        st = st_ref[...]
        new_pairs = []
        for pr in range(GLA_HEADS // 2):
            st_p = st[:, pr * LANES:(pr + 1) * LANES]
            st_pb = st_p.astype(BF16)
            kd_p = k_dec[:, pr * LANES:(pr + 1) * LANES]
            upd = []
            for hh in range(2):
                h = 2 * pr + hh
                keep = first_half if hh == 0 else jnp.logical_not(first_half)
                v_h = v[:, h * GLA_DV:(h + 1) * GLA_DV]
                qd_p = q_dec[:, pr * LANES:(pr + 1) * LANES]
                qdm = jnp.where(keep, qd_p, jnp.zeros_like(qd_p))
                o = _dot(a[h].astype(BF16), v_h) + _dot_nt(qdm, st_pb)
                ms = jnp.mean(o * o, axis=-1, keepdims=True)
                on = o * lax.rsqrt(ms + EPS) * nw_ref[...]
                rg = r_ref[0, pl.ds(r0, c), h * GLA_DV:(h + 1) * GLA_DV].astype(F32)
                o_ref[0, pl.ds(r0, c), h * GLA_DV:(h + 1) * GLA_DV] = (
                    on * (rg * _sigmoid(rg))).astype(o_ref.dtype)
                upd.append(_dot_tn(v_h, kd_p))
            new_pairs.append(st_decay[:, pr * LANES:(pr + 1) * LANES] * st_p
                             + jnp.where(lane_st, upd[0], upd[1]))
        st_ref[...] = jnp.concatenate(new_pairs, axis=1)
        return carry

    lax.fori_loop(0, n_chunks, chunk, 0)


def _gla(qk, v, g1, r, wg2, bg, nw, tg=512):
    b, t, _ = qk.shape
    sel_np, mask_np = _gla_consts()
    sel = jnp.asarray(sel_np, BF16)
    masks = jnp.asarray(mask_np, F32)
    kern = functools.partial(_gla_kernel, n_chunks=tg // GLA_CHUNK)
    seq = lambda w: pl.BlockSpec((1, tg, w), lambda bi, i: (bi, i, 0))
    return pl.pallas_call(
        kern,
        grid=(b, t // tg),
        in_specs=[seq(512), seq(512), seq(LANES), seq(512),
                  _const_spec((LANES, 256)), _const_spec((1, 256)), _const_spec((1, GLA_DV)),
                  _const_spec(sel.shape), _const_spec(masks.shape)],
        out_specs=seq(512),
        out_shape=jax.ShapeDtypeStruct((b, t, D_MIX), BF16),
        scratch_shapes=[pltpu.VMEM((GLA_DV, GLA_HEADS * GLA_DK), F32)],
        compiler_params=_params("arbitrary", "arbitrary"),
        name="gla",
    )(qk, v, g1, r, wg2, bg, nw, sel, masks)


def _moba_consts():
    blk = MOBA_BLOCK
    slopes = 2.0 ** (-8.0 * (np.arange(MOBA_HEADS) + 1) / MOBA_HEADS)
    s = np.arange(blk)[:, None]
    t = np.arange(blk)[None, :]
    dist = (t - s).astype(np.float32)
    bias = -slopes[:, None, None] * dist[None]
    bias_own = np.where((s <= t)[None], bias, NEG)
    return slopes, bias.astype(np.float32), bias_own.astype(np.float32)


def _moba_kernel(q_ref, k_ref, v_ref, bias_ref, bias_own_ref, o_ref,
                 kmean_ref, vt_ref, acc_ref, m_ref, l_ref, sel_ref, qm_ref, *, nb, slopes):
    blk = MOBA_BLOCK
    i = pl.program_id(1)

    @pl.when(i == 0)
    def _():
        for n in range(nb):
            kb = k_ref[0, n * blk:(n + 1) * blk, :].astype(F32)
            kmean_ref[n:n + 1, :] = jnp.sum(kb, axis=0, keepdims=True) * (1.0 / blk)
            vb = v_ref[0, n * blk:(n + 1) * blk, :].astype(F32)
            vt_ref[:, n * blk:(n + 1) * blk] = vb.T.astype(BF16)

    lane = lax.broadcasted_iota(jnp.int32, (blk, LANES), 1)
    first_half = lane < MOBA_DH
    rowid = lax.broadcasted_iota(jnp.int32, (nb, blk), 0)

    for h in range(MOBA_HEADS):
        pr = h // 2
        qp = q_ref[0, :, pr * LANES:(pr + 1) * LANES] * jnp.asarray(MOBA_DH ** -0.5, BF16)
        keep = first_half if h % 2 == 0 else jnp.logical_not(first_half)
        qm = jnp.where(keep, qp, jnp.zeros_like(qp))
        qm_ref[h] = qm
        km = kmean_ref[:, pr * LANES:(pr + 1) * LANES]
        km_hi = km.astype(BF16)
        km_lo = (km - km_hi.astype(F32)).astype(BF16)
        g = _dot_nt(km_hi, qm) + _dot_nt(km_lo, qm)
        g = jnp.where(rowid < i, g, NEG)
        sel = jnp.zeros((nb, blk), F32)
        for r in range(MOBA_TOPK):
            mx = jnp.max(g, axis=0, keepdims=True)
            idx = jnp.min(jnp.where(g == mx, rowid, nb), axis=0, keepdims=True)
            pick = rowid == idx
            sel = jnp.where(jnp.logical_and(pick, i > r), 1.0, sel)
            g = jnp.where(pick, -3.0e38, g)
        sel_ref[h] = sel

    acc_ref[...] = jnp.zeros_like(acc_ref)
    m_ref[...] = jnp.full_like(m_ref, NEG)
    l_ref[...] = jnp.zeros_like(l_ref)

    def head_step(h, n, kblk, own):
        pr = h // 2
        s = _dot_nt(kblk[:, pr * LANES:(pr + 1) * LANES], qm_ref[h])
        z = s + (bias_own_ref[h] if own else bias_ref[h])
        mb = jnp.max(z, axis=0, keepdims=True)
        m_old = m_ref[h:h + 1, :]
        if own:
            m_new = jnp.maximum(m_old, mb)
            off = m_new
        else:
            shift = (-slopes[h] * blk) * (i - n).astype(F32)
            picked = sel_ref[h, pl.ds(n, 1), :] > 0.5
            m_new = jnp.where(picked, jnp.maximum(m_old, mb + shift), m_old)
            off = jnp.where(picked, m_new - shift, -NEG)
        p = jnp.exp(z - off)
        alpha = jnp.exp(m_old - m_new)
        l_ref[h:h + 1, :] = alpha * l_ref[h:h + 1, :] + jnp.sum(p, axis=0, keepdims=True)
        m_ref[h:h + 1, :] = m_new
        vt = vt_ref[pr * LANES:(pr + 1) * LANES, pl.ds(pl.multiple_of(n * blk, blk), blk)]
        pv = _dot(vt, p.astype(BF16))
        half = pv[(h % 2) * MOBA_DH:(h % 2 + 1) * MOBA_DH, :]
        rows = slice(h * MOBA_DH, (h + 1) * MOBA_DH)
        acc_ref[rows, :] = alpha * acc_ref[rows, :] + half

    def past(n, carry):
        kblk = k_ref[0, pl.ds(pl.multiple_of(n * blk, blk), blk), :]
        for h in range(MOBA_HEADS):
            head_step(h, n, kblk, False)
        return carry

    lax.fori_loop(0, i, past, 0)
    kown = k_ref[0, pl.ds(pl.multiple_of(i * blk, blk), blk), :]
    for h in range(MOBA_HEADS):
        head_step(h, i, kown, True)

    for h in range(MOBA_HEADS):
        rows = slice(h * MOBA_DH, (h + 1) * MOBA_DH)
        acc_ref[rows, :] = acc_ref[rows, :] / l_ref[h:h + 1, :]
    o_ref[0] = acc_ref[...].T.astype(o_ref.dtype)


def _moba(q, k, v):
    b, t, _ = q.shape
    blk = MOBA_BLOCK
    nb = t // blk
    slopes, bias, bias_own = _moba_consts()
    kern = functools.partial(_moba_kernel, nb=nb, slopes=tuple(float(s) for s in slopes))
    full = pl.BlockSpec((1, t, D_MIX), lambda bi, i: (bi, 0, 0))
    return pl.pallas_call(
        kern,
        grid=(b, nb),
        in_specs=[pl.BlockSpec((1, blk, D_MIX), lambda bi, i: (bi, i, 0)), full, full,
                  _const_spec(bias.shape), _const_spec(bias_own.shape)],
        out_specs=pl.BlockSpec((1, blk, D_MIX), lambda bi, i: (bi, i, 0)),
        out_shape=jax.ShapeDtypeStruct((b, t, D_MIX), BF16),
        scratch_shapes=[pltpu.VMEM((nb, D_MIX), F32),
                        pltpu.VMEM((D_MIX, t), BF16),
                        pltpu.VMEM((D_MIX, blk), F32),
                        pltpu.VMEM((MOBA_HEADS, blk), F32),
                        pltpu.VMEM((MOBA_HEADS, blk), F32),
                        pltpu.VMEM((MOBA_HEADS, nb, blk), F32),
                        pltpu.VMEM((MOBA_HEADS, blk, LANES), BF16)],
        compiler_params=_params("arbitrary", "arbitrary"),
        name="moba",
    )(q, k, v, jnp.asarray(bias), jnp.asarray(bias_own))


_POOL_SUB = 128


def _pool_bands():
    r = np.arange(_POOL_SUB)[:, None] + _POOL_SUB
    c = np.arange(2 * _POOL_SUB)[None, :]
    return np.stack([((r - c >= 0) & (r - c < w)).astype(np.float32) for w in POOL_WINDOWS])


def _mix_kernel(x_ref, oa_ref, oc_ref, pu_ref, halo_ref, gates_ref, band_ref, pw_ref, ps_ref,
                wa_ref, wb_ref, wc_ref, wo_ref, nw_ref, o_ref, *, seq_len):
    tm = x_ref.shape[0]
    sub = _POOL_SUB
    t0 = (pl.program_id(0) * tm) % seq_len
    halo = halo_ref[...]
    halo = jnp.where(t0 > 0, halo, jnp.zeros_like(halo))
    ext = jnp.concatenate([halo, pu_ref[...]], axis=0)
    rows = lax.broadcasted_iota(jnp.int32, (sub, 1), 0)
    ys = []
    for j in range(tm // sub):
        win = ext[j * sub:(j + 2) * sub, :]
        tpos = t0 + j * sub + rows
        yg = []
        for g, w in enumerate(POOL_WINDOWS):
            wg = win[:, g * POOL_GC:(g + 1) * POOL_GC]
            s = _dot(band_ref[g], wg)
            cnt = jnp.minimum(tpos + 1, w).astype(F32)
            p = s / cnt - wg[sub:, :].astype(F32)
            yg.append(_dot(p.astype(BF16), pw_ref[g]))
        ys.append(jnp.concatenate(yg, axis=1))
    y = (jnp.concatenate(ys, axis=0) * ps_ref[...]).astype(BF16)

    ya = _dot(oa_ref[...], wa_ref[...])
    yb = _dot(y, wb_ref[...])
    yc = _dot(oc_ref[...], wc_ref[...])
    d = D_MODEL
    mixed = (_sigmoid(gates_ref[:, 0:d].astype(F32)) * ya
             + _sigmoid(gates_ref[:, d:2 * d].astype(F32)) * yb
             + _sigmoid(gates_ref[:, 2 * d:3 * d].astype(F32)) * yc)
    m2 = _dot(mixed.astype(BF16), wo_ref[...])
    o_ref[...] = x_ref[...] + _rms(m2, nw_ref[...])


def _mix(x, oa, oc, pu, gates, pool_w, pool_scale, wa, wb, wc, wo, nw, seq_len):
    n = x.shape[0]
    tm = _ROW_TILE
    sub = _POOL_SUB
    bands = jnp.asarray(_pool_bands(), BF16)
    row = lambda w: pl.BlockSpec((tm, w), lambda i: (i, 0))
    halo = pl.BlockSpec((sub, D_MIX), lambda i: (jnp.maximum(i * (tm // sub) - 1, 0), 0))
    return pl.pallas_call(
        functools.partial(_mix_kernel, seq_len=seq_len),
        grid=(n // tm,),
        in_specs=[row(D_MODEL), row(D_MIX), row(D_MIX), row(D_MIX), halo, row(3 * D_MODEL),
                  _const_spec(bands.shape), _const_spec(pool_w.shape), _const_spec((1, D_MIX)),
                  _const_spec(wa.shape), _const_spec(wb.shape), _const_spec(wc.shape),
                  _const_spec(wo.shape), _const_spec((1, D_MODEL))],
        out_specs=row(D_MODEL),
        out_shape=jax.ShapeDtypeStruct((n, D_MODEL), F32),
        compiler_params=_params("arbitrary"),
        name="mix",
    )(x, oa, oc, pu, pu, gates, bands, pool_w, pool_scale.reshape(1, D_MIX),
      wa, wb, wc, wo, nw.reshape(1, D_MODEL))


def _ffn_kernel(x_ref, npre_ref, wg_ref, wu_ref, wd_ref, npost_ref, o_ref):
    x = x_ref[...]
    h = _rms(x, npre_ref[...]).astype(BF16)
    g = _dot(h, wg_ref[...])
    u = _dot(h, wu_ref[...])
    a = (g * _sigmoid(g) * u).astype(BF16)
    f = _dot(a, wd_ref[...])
    o_ref[...] = x + _rms(f, npost_ref[...])


def _ffn(x, npre, wg, wu, wd, npost):
    n = x.shape[0]
    tm = 256
    row = pl.BlockSpec((tm, D_MODEL), lambda i: (i, 0))
    return pl.pallas_call(
        _ffn_kernel,
        grid=(n // tm,),
        in_specs=[row, _const_spec((1, D_MODEL)), _const_spec(wg.shape), _const_spec(wu.shape),
                  _const_spec(wd.shape), _const_spec((1, D_MODEL))],
        out_specs=row,
        out_shape=jax.ShapeDtypeStruct((n, D_MODEL), F32),
        compiler_params=_params("arbitrary"),
        name="ffn",
    )(x, npre.reshape(1, D_MODEL), wg, wu, wd, npost.reshape(1, D_MODEL))


def _prep_w_in(w):
    sizes = (256, 256, 512, GLA_RANK, 512, 512, 512, 512, 512, 3 * D_MODEL)
    offs = np.concatenate([[0], np.cumsum(sizes)])
    gq, gk, gv, g1, gr, pu, mq, mk, mv, gates = [w[:, offs[j]:offs[j + 1]] for j in range(len(sizes))]
    g1p = jnp.pad(g1, ((0, 0), (0, LANES - GLA_RANK)))
    return jnp.concatenate([gq, gk, gv, gr, pu, mq, mk, mv, gates, g1p], axis=1).astype(BF16)


def kernel(x, norm_mix_pre, w_in, gla_w_g2, gla_b_g, gla_norm, pool_w, pool_scale, w_branch_a,
           w_branch_b, w_branch_c, w_out, norm_mix_post, norm_ffn_pre, ffn_w_gate, ffn_w_up,
           ffn_w_down, norm_ffn_post):
    b, t, d = x.shape
    n = b * t
    xf = x.reshape(n, d)
    seq = lambda a: a.reshape(b, t, a.shape[-1])
    for l in range(w_in.shape[0]):
        gqk, gv, gr, pu, mq, mk, mv, gates, g1 = _inproj(xf, norm_mix_pre[l], _prep_w_in(w_in[l]))
        wg2 = jnp.pad(gla_w_g2[l], ((0, LANES - GLA_RANK), (0, 0))).astype(BF16)
        oa = _gla(seq(gqk), seq(gv), seq(g1), seq(gr), wg2,
                  gla_b_g[l].reshape(1, -1), gla_norm[l].reshape(1, -1))
        oc = _moba(seq(mq), seq(mk), seq(mv))
        xf = _mix(xf, oa.reshape(n, D_MIX), oc.reshape(n, D_MIX), pu, gates,
                  pool_w[l].astype(BF16), pool_scale[l],
                  w_branch_a[l].astype(BF16), w_branch_b[l].astype(BF16),
                  w_branch_c[l].astype(BF16), w_out[l].astype(BF16), norm_mix_post[l], t)
        xf = _ffn(xf, norm_ffn_pre[l], ffn_w_gate[l].astype(BF16), ffn_w_up[l].astype(BF16),
                  ffn_w_down[l].astype(BF16), norm_ffn_post[l])
    return xf.reshape(b, t, d)
```

```python
import functools

import numpy as np
import jax
import jax.numpy as jnp
from jax import lax
from jax.experimental import pallas as pl
from jax.experimental.pallas import tpu as pltpu

F32 = jnp.float32
BF16 = jnp.bfloat16

D_MODEL = 1024
D_MIX = 512
GLA_HEADS = 4
GLA_DK = 64
GLA_DV = 128
GLA_RANK = 16
GLA_TAU = 16.0
GLA_CHUNK = 64
POOL_WINDOWS = (2, 4, 8, 16)
POOL_GC = 128
MOBA_HEADS = 8
MOBA_DH = 64
MOBA_BLOCK = 256
MOBA_TOPK = 3
D_FF = 2816
EPS = 1e-6
NEG = -1e30

LANES = 128
VMEM_LIMIT = 56 * 1024 * 1024

_IN_WIDTHS = (512, 512, 512, 512, 512, 512, 512, 3072, LANES)
LOG2E = 1.4426950408889634
_IN_SCALES = (1.0, 1.0, 1.0, 1.0, MOBA_DH ** -0.5 * LOG2E, 1.0, 1.0, 1.0, 1.0)
_ROW_TILE = 512


def _dot(a, b):
    return jnp.dot(a, b, preferred_element_type=F32)


def _dot_nt(a, b):
    return lax.dot_general(a, b, (((1,), (1,)), ((), ())), preferred_element_type=F32)


def _dot_tn(a, b):
    return lax.dot_general(a, b, (((0,), (0,)), ((), ())), preferred_element_type=F32)


def _rms(x, w):
    ms = jnp.mean(x * x, axis=-1, keepdims=True)
    return x * lax.rsqrt(ms + EPS) * w


def _sigmoid(x):
    return 1.0 / (1.0 + jnp.exp(-x))


def _params(*sem):
    return pltpu.CompilerParams(dimension_semantics=sem, vmem_limit_bytes=VMEM_LIMIT)


def _const_spec(shape):
    nd = len(shape)
    return pl.BlockSpec(shape, lambda *_: (0,) * nd, pipeline_mode=pl.Buffered(1))


def _inproj_kernel(x_ref, nw_ref, w_ref, *o_refs):
    h = _rms(x_ref[...], nw_ref[...]).astype(BF16)
    off = 0
    for o_ref, scale in zip(o_refs, _IN_SCALES):
        width = o_ref.shape[-1]
        for c0 in range(0, width, 512):
            cw = min(512, width - c0)
            acc = _dot(h, w_ref[:, off + c0:off + c0 + cw])
            if scale != 1.0:
                acc = acc * scale
            o_ref[:, c0:c0 + cw] = acc.astype(o_ref.dtype)
        off += width


def _inproj(x, nw, wcat):
    n = x.shape[0]
    tm = _ROW_TILE
    total = sum(_IN_WIDTHS)
    return pl.pallas_call(
        _inproj_kernel,
        grid=(n // tm,),
        in_specs=[pl.BlockSpec((tm, D_MODEL), lambda i: (i, 0)),
                  _const_spec((1, D_MODEL)),
                  _const_spec((D_MODEL, total))],
        out_specs=[pl.BlockSpec((tm, w), lambda i: (i, 0)) for w in _IN_WIDTHS],
        out_shape=[jax.ShapeDtypeStruct((n, w), BF16) for w in _IN_WIDTHS],
        compiler_params=_params("arbitrary"),
        name="inproj",
    )(x, nw.reshape(1, D_MODEL), wcat)


_GLA_HALVES = (32, 16, 8, 4, 2, 1)


def _gla_consts():
    c = GLA_CHUNK
    t = np.arange(c)
    tri = (t[None, :] <= t[:, None]).astype(np.float32)
    blocks = [tri]
    masks = []
    for m in _GLA_HALVES:
        ref = (t // (2 * m)) * (2 * m) + m - 1
        blocks.append(tri[ref])
        same = (t[:, None] // (2 * m)) == (t[None, :] // (2 * m))
        upper = (t[:, None] % (2 * m)) >= m
        lower = (t[None, :] % (2 * m)) < m
        masks.append((same & upper & lower).astype(np.float32))
    blocks.append(tri[np.full(c, c - 1)])
    masks.append(np.eye(c, dtype=np.float32))
    sel = np.concatenate(blocks, 0)
    return np.concatenate([sel, sel], 1), np.stack(masks)


def _gla_kernel(qk_ref, v_ref, g1_ref, r_ref, wg2_ref, bg_ref, nw_ref, sel_ref, mask_ref,
                o_ref, st_ref, *, n_chunks):
    c = GLA_CHUNK
    nlev = len(_GLA_HALVES)

    @pl.when(pl.program_id(1) == 0)
    def _():
        st_ref[...] = jnp.zeros_like(st_ref)

    lane = lax.broadcasted_iota(jnp.int32, (c, LANES), 1)
    first_half = lane < GLA_DK
    lane_st = lax.broadcasted_iota(jnp.int32, (GLA_DV, LANES), 1) < GLA_DK

    def chunk(ci, carry):
        r0 = pl.multiple_of(ci * c, c)
        qk = qk_ref[0, pl.ds(r0, c), :]
        q = qk[:, :256].astype(F32) * (GLA_DK ** -0.5)
        k = qk[:, 256:].astype(F32)
        v = v_ref[0, pl.ds(r0, c), :]
        logit = _dot(g1_ref[0, pl.ds(r0, c), :], wg2_ref[...]) + bg_ref[...]
        log_a = (jnp.minimum(logit, 0.0) - jnp.log(1.0 + jnp.exp(-jnp.abs(logit)))) * (1.0 / GLA_TAU)
        la_hi = log_a.astype(BF16)
        la_lo = (log_a - la_hi.astype(F32)).astype(BF16)
        rb = _dot(sel_ref[...], jnp.concatenate([la_hi, la_lo], axis=0))
        bc = rb[0:c]
        b_last = rb[(nlev + 1) * c:(nlev + 2) * c]
        q_dec = (q * jnp.exp(bc)).astype(BF16)
        k_dec = (k * jnp.exp(b_last - bc)).astype(BF16)
        st_decay = jnp.exp(b_last[0:1, :])

        a = [jnp.zeros((c, c), F32) for _ in range(GLA_HEADS)]
        for lv in range(nlev + 1):
            if lv < nlev:
                w = jnp.exp(-jnp.abs(bc - rb[(lv + 1) * c:(lv + 2) * c]))
                qs = (q * w).astype(BF16)
                ks = (k * w).astype(BF16)
            else:
                qs = q.astype(BF16)
                ks = k.astype(BF16)
            msk = mask_ref[lv]
            for h in range(GLA_HEADS):
                pr = h // 2
                qp = qs[:, pr * LANES:(pr + 1) * LANES]
                kp = ks[:, pr * LANES:(pr + 1) * LANES]
                keep = first_half if h % 2 == 0 else jnp.logical_not(first_half)
                qm = jnp.where(keep, qp, jnp.zeros_like(qp))
                a[h] = a[h] + _dot_nt(qm, kp) * msk

        st = st_ref[...]
        new_pairs = []
        for pr in range(GLA_HEADS // 2):
            st_p = st[:, pr * LANES:(pr + 1) * LANES]
            st_pb = st_p.astype(BF16)
            kd_p = k_dec[:, pr * LANES:(pr + 1) * LANES]
            upd = []
            for hh in range(2):
                h = 2 * pr + hh
                keep = first_half if hh == 0 else jnp.logical_not(first_half)
                v_h = v[:, h * GLA_DV:(h + 1) * GLA_DV]
                qd_p = q_dec[:, pr * LANES:(pr + 1) * LANES]
                qdm = jnp.where(keep, qd_p, jnp.zeros_like(qd_p))
                o = _dot(a[h].astype(BF16), v_h) + _dot_nt(qdm, st_pb)
                ms = jnp.mean(o * o, axis=-1, keepdims=True)
                on = o * lax.rsqrt(ms + EPS) * nw_ref[...]
                rg = r_ref[0, pl.ds(r0, c), h * GLA_DV:(h + 1) * GLA_DV].astype(F32)
                o_ref[0, pl.ds(r0, c), h * GLA_DV:(h + 1) * GLA_DV] = (
                    on * (rg * _sigmoid(rg))).astype(o_ref.dtype)
                upd.append(_dot_tn(v_h, kd_p))
            new_pairs.append(st_decay[:, pr * LANES:(pr + 1) * LANES] * st_p
                             + jnp.where(lane_st, upd[0], upd[1]))
        st_ref[...] = jnp.concatenate(new_pairs, axis=1)
        return carry

    lax.fori_loop(0, n_chunks, chunk, 0)


def _gla(qk, v, g1, r, wg2, bg, nw, tg=512):
    b, t, _ = qk.shape
    sel_np, mask_np = _gla_consts()
    sel = jnp.asarray(sel_np, BF16)
    masks = jnp.asarray(mask_np, F32)
    kern = functools.partial(_gla_kernel, n_chunks=tg // GLA_CHUNK)
    seq = lambda w: pl.BlockSpec((1, tg, w), lambda bi, i: (bi, i, 0))
    return pl.pallas_call(
        kern,
        grid=(b, t // tg),
        in_specs=[seq(512), seq(512), seq(LANES), seq(512),
                  _const_spec((LANES, 256)), _const_spec((1, 256)), _const_spec((1, GLA_DV)),
                  _const_spec(sel.shape), _const_spec(masks.shape)],
        out_specs=seq(512),
        out_shape=jax.ShapeDtypeStruct((b, t, D_MIX), BF16),
        scratch_shapes=[pltpu.VMEM((GLA_DV, GLA_HEADS * GLA_DK), F32)],
        compiler_params=_params("arbitrary", "arbitrary"),
        name="gla",
    )(qk, v, g1, r, wg2, bg, nw, sel, masks)


def _moba_consts():
    blk = MOBA_BLOCK
    slopes = 2.0 ** (-8.0 * (np.arange(MOBA_HEADS) + 1) / MOBA_HEADS)
    s = np.arange(blk)[:, None]
    t = np.arange(blk)[None, :]
    dist = (t - s).astype(np.float32)
    bias = -slopes[:, None, None] * dist[None] * LOG2E
    bias_own = np.where((s <= t)[None], bias, NEG)
    return slopes * LOG2E, np.stack([bias, bias_own]).astype(np.float32)


_MOBA_VROWS = MOBA_DH + 16


def _moba_kernel(q_ref, k_ref, v_ref, bias_ref, o_ref,
                 kmean_ref, vt_ref, acc_ref, m_ref, sel_ref, qm_ref, z_ref, bmax_ref,
                 *, nb, slopes):
    blk = MOBA_BLOCK
    i = pl.program_id(1)

    @pl.when(i == 0)
    def _():
        for n in range(nb):
            kb = k_ref[0, n * blk:(n + 1) * blk, :].astype(F32)
            kmean_ref[n:n + 1, :] = jnp.sum(kb, axis=0, keepdims=True) * (1.0 / blk)
            vbt = v_ref[0, n * blk:(n + 1) * blk, :].astype(F32).T
            for h in range(MOBA_HEADS):
                vt_ref[h, 0:MOBA_DH, n * blk:(n + 1) * blk] = (
                    vbt[h * MOBA_DH:(h + 1) * MOBA_DH, :].astype(BF16))
                vt_ref[h, MOBA_DH:_MOBA_VROWS, n * blk:(n + 1) * blk] = jnp.ones(
                    (_MOBA_VROWS - MOBA_DH, blk), BF16)

    lane = lax.broadcasted_iota(jnp.int32, (blk, LANES), 1)
    first_half = lane < MOBA_DH
    rowid = lax.broadcasted_iota(jnp.int32, (nb, blk), 0)

    for h in range(MOBA_HEADS):
        pr = h // 2
        qp = q_ref[0, :, pr * LANES:(pr + 1) * LANES]
        keep = first_half if h % 2 == 0 else jnp.logical_not(first_half)
        qm = jnp.where(keep, qp, jnp.zeros_like(qp))
        qm_ref[h] = qm
        km = kmean_ref[:, pr * LANES:(pr + 1) * LANES]
        km_hi = km.astype(BF16)
        km_lo = (km - km_hi.astype(F32)).astype(BF16)
        g = _dot_nt(km_hi, qm) + _dot_nt(km_lo, qm)
        g = jnp.where(rowid < i, g, NEG)
        sel = jnp.where(rowid == i, 1.0, 0.0)
        for r in range(MOBA_TOPK):
            mx = jnp.max(g, axis=0, keepdims=True)
            idx = jnp.min(jnp.where(g == mx, rowid, nb), axis=0, keepdims=True)
            pick = rowid == idx
            sel = jnp.where(jnp.logical_and(pick, i > r), 1.0, sel)
            g = jnp.where(pick, -3.0e38, g)
        sel_ref[h] = sel

    acc_ref[...] = jnp.zeros_like(acc_ref)
    m_ref[...] = jnp.full_like(m_ref, NEG)

    def scores(n, slot):
        kblk = k_ref[0, pl.ds(pl.multiple_of(n * blk, blk), blk), :]
        own = (n == i).astype(jnp.int32)
        for h in range(MOBA_HEADS):
            pr = h // 2
            z = _dot_nt(kblk[:, pr * LANES:(pr + 1) * LANES], qm_ref[h]) + bias_ref[own, h]
            z_ref[slot, h] = z
            bmax_ref[slot, h:h + 1, :] = jnp.max(z, axis=0, keepdims=True)

    def accumulate(n, slot):
        for h in range(MOBA_HEADS):
            shift = (-slopes[h] * blk) * (i - n).astype(F32)
            picked = sel_ref[h, pl.ds(n, 1), :] > 0.5
            m_old = m_ref[h:h + 1, :]
            m_new = jnp.where(picked, jnp.maximum(m_old, bmax_ref[slot, h:h + 1, :] + shift), m_old)
            off = jnp.where(picked, m_new - shift, -NEG)
            p = jnp.exp2(z_ref[slot, h] - off)
            alpha = jnp.exp2(m_old - m_new)
            m_ref[h:h + 1, :] = m_new
            vt = vt_ref[h, :, pl.ds(pl.multiple_of(n * blk, blk), blk)]
            acc_ref[h] = alpha * acc_ref[h] + _dot(vt, p.astype(BF16))

    n_pairs = (i + 2) // 2
    scores(0, 0)

    def pair(j, carry):
        n0 = 2 * j
        scores(n0 + 1, 1)
        accumulate(n0, 0)
        scores(n0 + 2, 0)
        accumulate(n0 + 1, 1)
        return carry

    lax.fori_loop(0, n_pairs - 1, pair, 0)
    n0 = 2 * (n_pairs - 1)
    scores(n0 + 1, 1)
    accumulate(n0, 0)
    accumulate(n0 + 1, 1)

    out_t = jnp.concatenate(
        [acc_ref[h, 0:MOBA_DH, :] / acc_ref[h, MOBA_DH:MOBA_DH + 1, :] for h in range(MOBA_HEADS)], axis=0)
    o_ref[0] = out_t.T.astype(o_ref.dtype)


def _moba(q, k, v):
    b, t, _ = q.shape
    blk = MOBA_BLOCK
    nb = t // blk
    assert nb % 2 == 0
    slopes, bias = _moba_consts()
    kern = functools.partial(_moba_kernel, nb=nb, slopes=tuple(float(s) for s in slopes))
    full = pl.BlockSpec((1, t, D_MIX), lambda bi, i: (bi, 0, 0))
    return pl.pallas_call(
        kern,
        grid=(b, nb),
        in_specs=[pl.BlockSpec((1, blk, D_MIX), lambda bi, i: (bi, i, 0)), full, full,
                  _const_spec(bias.shape)],
        out_specs=pl.BlockSpec((1, blk, D_MIX), lambda bi, i: (bi, i, 0)),
        out_shape=jax.ShapeDtypeStruct((b, t, D_MIX), BF16),
        scratch_shapes=[pltpu.VMEM((nb, D_MIX), F32),
                        pltpu.VMEM((MOBA_HEADS, _MOBA_VROWS, t), BF16),
                        pltpu.VMEM((MOBA_HEADS, _MOBA_VROWS, blk), F32),
                        pltpu.VMEM((MOBA_HEADS, blk), F32),
                        pltpu.VMEM((MOBA_HEADS, nb, blk), F32),
                        pltpu.VMEM((MOBA_HEADS, blk, LANES), BF16),
                        pltpu.VMEM((2, MOBA_HEADS, blk, blk), F32),
                        pltpu.VMEM((2, MOBA_HEADS, blk), F32)],
        compiler_params=_params("arbitrary", "arbitrary"),
        name="moba",
    )(q, k, v, jnp.asarray(bias))


_POOL_SUB = 128


def _pool_bands():
    r = np.arange(_POOL_SUB)[:, None] + _POOL_SUB
    c = np.arange(2 * _POOL_SUB)[None, :]
    return np.stack([((r - c >= 0) & (r - c < w)).astype(np.float32) for w in POOL_WINDOWS])


def _mix_kernel(x_ref, oa_ref, oc_ref, pu_ref, halo_ref, gates_ref, band_ref, pw_ref, ps_ref,
                wa_ref, wb_ref, wc_ref, wo_ref, nw_ref, o_ref, *, seq_len):
    tm = x_ref.shape[0]
    sub = _POOL_SUB
    t0 = (pl.program_id(0) * tm) % seq_len
    halo = halo_ref[...]
    halo = jnp.where(t0 > 0, halo, jnp.zeros_like(halo))
    ext = jnp.concatenate([halo, pu_ref[...]], axis=0)
    ya = _dot(oa_ref[...], wa_ref[...])
    yc = _dot(oc_ref[...], wc_ref[...])

    n_sub = tm // sub
    sums = [[_dot(band_ref[g], ext[j * sub:(j + 2) * sub, g * POOL_GC:(g + 1) * POOL_GC])
             for g in range(len(POOL_WINDOWS))] for j in range(n_sub)]
    tpos = t0 + lax.broadcasted_iota(jnp.int32, (tm, 1), 0)
    yg = []
    for g, w in enumerate(POOL_WINDOWS):
        s = jnp.concatenate([sums[j][g] for j in range(n_sub)], axis=0)
        cnt = jnp.minimum(tpos + 1, w).astype(F32)
        p = s / cnt - pu_ref[:, g * POOL_GC:(g + 1) * POOL_GC].astype(F32)
        yg.append(_dot(p.astype(BF16), pw_ref[g]))
    y = (jnp.concatenate(yg, axis=1) * ps_ref[...]).astype(BF16)
    yb = _dot(y, wb_ref[...])
    d = D_MODEL
    mixed = (_sigmoid(gates_ref[:, 0:d].astype(F32)) * ya
             + _sigmoid(gates_ref[:, d:2 * d].astype(F32)) * yb
             + _sigmoid(gates_ref[:, 2 * d:3 * d].astype(F32)) * yc)
    m2 = _dot(mixed.astype(BF16), wo_ref[...])
    o_ref[...] = x_ref[...] + _rms(m2, nw_ref[...])


def _mix(x, oa, oc, pu, gates, pool_w, pool_scale, wa, wb, wc, wo, nw, seq_len):
    n = x.shape[0]
    tm = _ROW_TILE
    sub = _POOL_SUB
    bands = jnp.asarray(_pool_bands(), BF16)
    row = lambda w: pl.BlockSpec((tm, w), lambda i: (i, 0))
    halo = pl.BlockSpec((sub, D_MIX), lambda i: (jnp.maximum(i * (tm // sub) - 1, 0), 0))
    return pl.pallas_call(
        functools.partial(_mix_kernel, seq_len=seq_len),
        grid=(n // tm,),
        in_specs=[row(D_MODEL), row(D_MIX), row(D_MIX), row(D_MIX), halo, row(3 * D_MODEL),
                  _const_spec(bands.shape), _const_spec(pool_w.shape), _const_spec((1, D_MIX)),
                  _const_spec(wa.shape), _const_spec(wb.shape), _const_spec(wc.shape),
                  _const_spec(wo.shape), _const_spec((1, D_MODEL))],
        out_specs=row(D_MODEL),
        out_shape=jax.ShapeDtypeStruct((n, D_MODEL), F32),
        compiler_params=_params("arbitrary"),
        name="mix",
    )(x, oa, oc, pu, pu, gates, bands, pool_w, pool_scale.reshape(1, D_MIX),
      wa, wb, wc, wo, nw.reshape(1, D_MODEL))


def _ffn_kernel(x_ref, npre_ref, wg_ref, wu_ref, wd_ref, npost_ref, o_ref):
    x = x_ref[...]
    h = _rms(x, npre_ref[...]).astype(BF16)
    g = _dot(h, wg_ref[...])
    u = _dot(h, wu_ref[...])
    a = (g * _sigmoid(g) * u).astype(BF16)
    f = _dot(a, wd_ref[...])
    o_ref[...] = x + _rms(f, npost_ref[...])


def _ffn(x, npre, wg, wu, wd, npost):
    n = x.shape[0]
    tm = 256
    row = pl.BlockSpec((tm, D_MODEL), lambda i: (i, 0))
    return pl.pallas_call(
        _ffn_kernel,
        grid=(n // tm,),
        in_specs=[row, _const_spec((1, D_MODEL)), _const_spec(wg.shape), _const_spec(wu.shape),
                  _const_spec(wd.shape), _const_spec((1, D_MODEL))],
        out_specs=row,
        out_shape=jax.ShapeDtypeStruct((n, D_MODEL), F32),
        compiler_params=_params("arbitrary"),
        name="ffn",
    )(x, npre.reshape(1, D_MODEL), wg, wu, wd, npost.reshape(1, D_MODEL))


def _prep_w_in(w):
    sizes = (256, 256, 512, GLA_RANK, 512, 512, 512, 512, 512, 3 * D_MODEL)
    offs = np.concatenate([[0], np.cumsum(sizes)])
    gq, gk, gv, g1, gr, pu, mq, mk, mv, gates = [w[:, offs[j]:offs[j + 1]] for j in range(len(sizes))]
    g1p = jnp.pad(g1, ((0, 0), (0, LANES - GLA_RANK)))
    return jnp.concatenate([gq, gk, gv, gr, pu, mq, mk, mv, gates, g1p], axis=1).astype(BF16)


def kernel(x, norm_mix_pre, w_in, gla_w_g2, gla_b_g, gla_norm, pool_w, pool_scale, w_branch_a,
           w_branch_b, w_branch_c, w_out, norm_mix_post, norm_ffn_pre, ffn_w_gate, ffn_w_up,
           ffn_w_down, norm_ffn_post):
    b, t, d = x.shape
    n = b * t
    xf = x.reshape(n, d)
    seq = lambda a: a.reshape(b, t, a.shape[-1])
    for l in range(w_in.shape[0]):
        gqk, gv, gr, pu, mq, mk, mv, gates, g1 = _inproj(xf, norm_mix_pre[l], _prep_w_in(w_in[l]))
        wg2 = jnp.pad(gla_w_g2[l], ((0, LANES - GLA_RANK), (0, 0))).astype(BF16)
        oa = _gla(seq(gqk), seq(gv), seq(g1), seq(gr), wg2,
                  gla_b_g[l].reshape(1, -1), gla_norm[l].reshape(1, -1))
        oc = _moba(seq(mq), seq(mk), seq(mv))
        xf = _mix(xf, oa.reshape(n, D_MIX), oc.reshape(n, D_MIX), pu, gates,
                  pool_w[l].astype(BF16), pool_scale[l],
                  w_branch_a[l].astype(BF16), w_branch_b[l].astype(BF16),
                  w_branch_c[l].astype(BF16), w_out[l].astype(BF16), norm_mix_post[l], t)
        xf = _ffn(xf, norm_ffn_pre[l], ffn_w_gate[l].astype(BF16), ffn_w_up[l].astype(BF16),
                  ffn_w_down[l].astype(BF16), norm_ffn_post[l])
    return xf.reshape(b, t, d)
```

```python
import functools

import numpy as np
import jax
import jax.numpy as jnp
from jax import lax
from jax.experimental import pallas as pl
from jax.experimental.pallas import tpu as pltpu

F32 = jnp.float32
BF16 = jnp.bfloat16

D_MODEL = 1024
D_MIX = 512
GLA_HEADS = 4
GLA_DK = 64
GLA_DV = 128
GLA_RANK = 16
GLA_TAU = 16.0
GLA_CHUNK = 64
POOL_WINDOWS = (2, 4, 8, 16)
POOL_GC = 128
MOBA_HEADS = 8
MOBA_DH = 64
MOBA_BLOCK = 256
MOBA_TOPK = 3
D_FF = 2816
EPS = 1e-6
NEG = -1e30

LANES = 128
VMEM_LIMIT = 56 * 1024 * 1024

_IN_WIDTHS = (512, 512, 512, 512, 512, 512, 512, 3072, LANES)
LOG2E = 1.4426950408889634
_IN_SCALES = (1.0, 1.0, 1.0, 1.0, MOBA_DH ** -0.5 * LOG2E, 1.0, 1.0, 1.0, 1.0)
_ROW_TILE = 512


def _dot(a, b):
    return jnp.dot(a, b, preferred_element_type=F32)


def _dot_nt(a, b):
    return lax.dot_general(a, b, (((1,), (1,)), ((), ())), preferred_element_type=F32)


def _dot_tn(a, b):
    return lax.dot_general(a, b, (((0,), (0,)), ((), ())), preferred_element_type=F32)


def _rms(x, w):
    ms = jnp.mean(x * x, axis=-1, keepdims=True)
    return x * lax.rsqrt(ms + EPS) * w


def _sigmoid(x):
    return 1.0 / (1.0 + jnp.exp(-x))


def _params(*sem):
    return pltpu.CompilerParams(dimension_semantics=sem, vmem_limit_bytes=VMEM_LIMIT)


def _const_spec(shape):
    nd = len(shape)
    return pl.BlockSpec(shape, lambda *_: (0,) * nd, pipeline_mode=pl.Buffered(1))


def _inproj_kernel(x_ref, nw_ref, w_ref, *o_refs):
    h = _rms(x_ref[...], nw_ref[...]).astype(BF16)
    off = 0
    for o_ref, scale in zip(o_refs, _IN_SCALES):
        width = o_ref.shape[-1]
        for c0 in range(0, width, 512):
            cw = min(512, width - c0)
            acc = _dot(h, w_ref[:, off + c0:off + c0 + cw])
            if scale != 1.0:
                acc = acc * scale
            o_ref[:, c0:c0 + cw] = acc.astype(o_ref.dtype)
        off += width


def _inproj(x, nw, wcat):
    n = x.shape[0]
    tm = _ROW_TILE
    total = sum(_IN_WIDTHS)
    return pl.pallas_call(
        _inproj_kernel,
        grid=(n // tm,),
        in_specs=[pl.BlockSpec((tm, D_MODEL), lambda i: (i, 0)),
                  _const_spec((1, D_MODEL)),
                  _const_spec((D_MODEL, total))],
        out_specs=[pl.BlockSpec((tm, w), lambda i: (i, 0)) for w in _IN_WIDTHS],
        out_shape=[jax.ShapeDtypeStruct((n, w), BF16) for w in _IN_WIDTHS],
        compiler_params=_params("arbitrary"),
        name="inproj",
    )(x, nw.reshape(1, D_MODEL), wcat)


_GLA_HALVES = (32, 16, 8, 4, 2, 1)


def _gla_consts():
    c = GLA_CHUNK
    t = np.arange(c)
    tri = (t[None, :] <= t[:, None]).astype(np.float32)
    blocks = [tri]
    masks = []
    for m in _GLA_HALVES:
        ref = (t // (2 * m)) * (2 * m) + m - 1
        blocks.append(tri[ref])
        same = (t[:, None] // (2 * m)) == (t[None, :] // (2 * m))
        upper = (t[:, None] % (2 * m)) >= m
        lower = (t[None, :] % (2 * m)) < m
        masks.append((same & upper & lower).astype(np.float32))
    blocks.append(tri[np.full(c, c - 1)])
    masks.append(np.eye(c, dtype=np.float32))
    sel = np.concatenate(blocks, 0)
    return np.concatenate([sel, sel], 1), np.stack(masks)


def _gla_kernel(qk_ref, v_ref, g1_ref, r_ref, wg2_ref, bg_ref, nw_ref, sel_ref, mask_ref,
                o_ref, st_ref, *, n_chunks):
    c = GLA_CHUNK
    nlev = len(_GLA_HALVES)

    @pl.when(pl.program_id(1) == 0)
    def _():
        st_ref[...] = jnp.zeros_like(st_ref)

    lane = lax.broadcasted_iota(jnp.int32, (c, LANES), 1)
    first_half = lane < GLA_DK
    lane_st = lax.broadcasted_iota(jnp.int32, (GLA_DV, LANES), 1) < GLA_DK

    chunks = range(n_chunks)
    heads = range(GLA_HEADS)
    keep = [first_half if h % 2 == 0 else jnp.logical_not(first_half) for h in heads]

    def pair(x, h):
        return x[:, (h // 2) * LANES:(h // 2 + 1) * LANES]

    def masked(x, h):
        xp = pair(x, h)
        return jnp.where(keep[h], xp, jnp.zeros_like(xp))

    qk = qk_ref[0]
    q_all = qk[:, :256].astype(F32) * (GLA_DK ** -0.5)
    k_all = qk[:, 256:].astype(F32)
    logit = _dot(g1_ref[0], wg2_ref[...]) + bg_ref[...]
    log_a = (jnp.minimum(logit, 0.0) - jnp.log(1.0 + jnp.exp(-jnp.abs(logit)))) * (1.0 / GLA_TAU)
    la_hi = log_a.astype(BF16)
    la_lo = (log_a - la_hi.astype(F32)).astype(BF16)
    rows = [slice(ci * c, (ci + 1) * c) for ci in chunks]
    q = [q_all[r] for r in rows]
    k = [k_all[r] for r in rows]
    v = [v_ref[0, r, :] for r in rows]
    rb = [_dot(sel_ref[...], jnp.concatenate([la_hi[r], la_lo[r]], axis=0)) for r in rows]
    bc = [x[0:c] for x in rb]
    b_last = [x[(nlev + 1) * c:(nlev + 2) * c] for x in rb]
    q_dec = [(q[ci] * jnp.exp(bc[ci])).astype(BF16) for ci in chunks]
    k_dec = [(k[ci] * jnp.exp(b_last[ci] - bc[ci])).astype(BF16) for ci in chunks]
    st_decay = [jnp.exp(b_last[ci][0:1, :]) for ci in chunks]

    a = [[jnp.zeros((c, c), F32) for _ in heads] for _ in chunks]
    for lv in range(nlev + 1):
        msk = mask_ref[lv]
        for ci in chunks:
            if lv < nlev:
                w = jnp.exp(-jnp.abs(bc[ci] - rb[ci][(lv + 1) * c:(lv + 2) * c]))
                qs = (q[ci] * w).astype(BF16)
                ks = (k[ci] * w).astype(BF16)
            else:
                qs = q[ci].astype(BF16)
                ks = k[ci].astype(BF16)
            for h in heads:
                a[ci][h] = a[ci][h] + _dot_nt(masked(qs, h), pair(ks, h)) * msk

    v_h = [[v[ci][:, h * GLA_DV:(h + 1) * GLA_DV] for h in heads] for ci in chunks]
    o_intra = [[_dot(a[ci][h].astype(BF16), v_h[ci][h]) for h in heads] for ci in chunks]
    upd = [[_dot_tn(v_h[ci][h], pair(k_dec[ci], h)) for h in heads] for ci in chunks]

    st = st_ref[...]
    for ci in chunks:
        st_b = st.astype(BF16)
        for h in heads:
            o = o_intra[ci][h] + _dot_nt(masked(q_dec[ci], h), pair(st_b, h))
            ms = jnp.mean(o * o, axis=-1, keepdims=True)
            on = o * lax.rsqrt(ms + EPS) * nw_ref[...]
            rg = r_ref[0, rows[ci], h * GLA_DV:(h + 1) * GLA_DV].astype(F32)
            o_ref[0, rows[ci], h * GLA_DV:(h + 1) * GLA_DV] = (on * (rg * _sigmoid(rg))).astype(o_ref.dtype)
        st = st_decay[ci] * st + jnp.concatenate(
            [jnp.where(lane_st, upd[ci][2 * pr], upd[ci][2 * pr + 1]) for pr in range(GLA_HEADS // 2)],
            axis=1)
    st_ref[...] = st


def _gla(qk, v, g1, r, wg2, bg, nw, tg=512):
    b, t, _ = qk.shape
    sel_np, mask_np = _gla_consts()
    sel = jnp.asarray(sel_np, BF16)
    masks = jnp.asarray(mask_np, F32)
    kern = functools.partial(_gla_kernel, n_chunks=tg // GLA_CHUNK)
    seq = lambda w: pl.BlockSpec((1, tg, w), lambda bi, i: (bi, i, 0))
    return pl.pallas_call(
        kern,
        grid=(b, t // tg),
        in_specs=[seq(512), seq(512), seq(LANES), seq(512),
                  _const_spec((LANES, 256)), _const_spec((1, 256)), _const_spec((1, GLA_DV)),
                  _const_spec(sel.shape), _const_spec(masks.shape)],
        out_specs=seq(512),
        out_shape=jax.ShapeDtypeStruct((b, t, D_MIX), BF16),
        scratch_shapes=[pltpu.VMEM((GLA_DV, GLA_HEADS * GLA_DK), F32)],
        compiler_params=_params("arbitrary", "arbitrary"),
        name="gla",
    )(qk, v, g1, r, wg2, bg, nw, sel, masks)


def _moba_consts():
    blk = MOBA_BLOCK
    c = (2.0 ** (-8.0 * (np.arange(MOBA_HEADS) + 1) / MOBA_HEADS) * LOG2E).astype(np.float32)
    pos = np.arange(blk, dtype=np.float32)
    kx = np.zeros((blk, LANES), np.float32)
    kx[:, 0:3] = pos[:, None]
    pieces = []
    rest = c.copy()
    for _ in range(3):
        piece = rest.astype(jnp.bfloat16).astype(np.float32)
        pieces.append(piece)
        rest = rest - piece
    qx = np.zeros((MOBA_HEADS, blk, LANES), np.float32)
    for j, piece in enumerate(pieces):
        qx[:, :, j] = piece[:, None]
    causal = np.where(pos[:, None] <= pos[None, :], 0.0, NEG).astype(np.float32)
    ucol = c[:, None] * pos[None, :]
    return c, kx, qx, causal, ucol.astype(np.float32)


_MOBA_VROWS = MOBA_DH + 16


def _moba_kernel(q_ref, k_ref, v_ref, kx_ref, qx_ref, causal_ref, ucol_ref, o_ref,
                 kmean_ref, vt_ref, acc_ref, m_ref, sel_ref, qa_ref, z_ref, bmax_ref,
                 *, nb, slopes):
    blk = MOBA_BLOCK
    i = pl.program_id(1)

    @pl.when(i == 0)
    def _():
        for n in range(nb):
            kb = k_ref[0, n * blk:(n + 1) * blk, :].astype(F32)
            kmean_ref[n:n + 1, :] = jnp.sum(kb, axis=0, keepdims=True) * (1.0 / blk)
            vbt = v_ref[0, n * blk:(n + 1) * blk, :].astype(F32).T
            for h in range(MOBA_HEADS):
                vt_ref[h, 0:MOBA_DH, n * blk:(n + 1) * blk] = (
                    vbt[h * MOBA_DH:(h + 1) * MOBA_DH, :].astype(BF16))
                vt_ref[h, MOBA_DH:_MOBA_VROWS, n * blk:(n + 1) * blk] = jnp.ones(
                    (_MOBA_VROWS - MOBA_DH, blk), BF16)

    lane = lax.broadcasted_iota(jnp.int32, (blk, LANES), 1)
    first_half = lane < MOBA_DH
    rowid = lax.broadcasted_iota(jnp.int32, (nb, blk), 0)

    for h in range(MOBA_HEADS):
        pr = h // 2
        qp = q_ref[0, :, pr * LANES:(pr + 1) * LANES]
        keep = first_half if h % 2 == 0 else jnp.logical_not(first_half)
        qm = jnp.where(keep, qp, jnp.zeros_like(qp))
        qa_ref[h, :, 0:LANES] = qm
        qa_ref[h, :, LANES:2 * LANES] = qx_ref[h]
        km = kmean_ref[:, pr * LANES:(pr + 1) * LANES]
        km_hi = km.astype(BF16)
        km_lo = (km - km_hi.astype(F32)).astype(BF16)
        g = _dot_nt(km_hi, qm) + _dot_nt(km_lo, qm)
        g = jnp.where(rowid < i, g, NEG)
        sel = jnp.where(rowid == i, 1.0, 0.0)
        for r in range(MOBA_TOPK):
            mx = jnp.max(g, axis=0, keepdims=True)
            idx = jnp.min(jnp.where(g == mx, rowid, nb), axis=0, keepdims=True)
            pick = rowid == idx
            sel = jnp.where(jnp.logical_and(pick, i > r), 1.0, sel)
            g = jnp.where(pick, -3.0e38, g)
        sel_ref[h] = sel

    acc_ref[...] = jnp.zeros_like(acc_ref)
    m_ref[...] = jnp.full_like(m_ref, NEG)

    def scores(h, n, slot, own):
        pr = h // 2
        kp = k_ref[0, pl.ds(pl.multiple_of(n * blk, blk), blk), pr * LANES:(pr + 1) * LANES]
        z = _dot_nt(jnp.concatenate([kp, kx_ref[...]], axis=1), qa_ref[h])
        if own:
            z = z + causal_ref[...]
        z_ref[slot, h] = z
        bmax_ref[slot, h:h + 1, :] = jnp.max(z, axis=0, keepdims=True)

    def accumulate(h, n, slot):
        u = ucol_ref[h:h + 1, :] + (slopes[h] * blk) * (i - n).astype(F32)
        picked = sel_ref[h, pl.ds(n, 1), :] > 0.5
        m_old = m_ref[h:h + 1, :]
        m_new = jnp.where(picked, jnp.maximum(m_old, bmax_ref[slot, h:h + 1, :] - u), m_old)
        off = jnp.where(picked, m_new + u, -NEG)
        p = jnp.exp2(z_ref[slot, h] - off)
        alpha = jnp.exp2(m_old - m_new)
        m_ref[h:h + 1, :] = m_new
        vt = vt_ref[h, :, pl.ds(pl.multiple_of(n * blk, blk), blk)]
        acc_ref[h] = alpha * acc_ref[h] + _dot(vt, p.astype(BF16))

    def step(score_args, acc_args):
        for h in range(MOBA_HEADS):
            if score_args is not None:
                scores(h, *score_args)
            if acc_args is not None:
                accumulate(h, *acc_args)

    @pl.when(i == 0)
    def _():
        step((0, 0, True), None)
        step(None, (0, 0))

    @pl.when(i > 0)
    def _():
        step((0, 0, False), None)

    def two_steps(j, carry):
        n1 = 2 * j + 1
        step((n1, 1, False), (n1 - 1, 0))
        step((n1 + 1, 0, False), (n1, 1))
        return carry

    lax.fori_loop(0, (i - 1) // 2, two_steps, 0)

    @pl.when(i % 2 == 1)
    def _():
        step((i, 1, True), (i - 1, 0))
        step(None, (i, 1))

    @pl.when(jnp.logical_and(i % 2 == 0, i > 0))
    def _():
        step((i - 1, 1, False), (i - 2, 0))
        step((i, 0, True), (i - 1, 1))
        step(None, (i, 0))

    out_t = jnp.concatenate(
        [acc_ref[h, 0:MOBA_DH, :] / acc_ref[h, MOBA_DH:MOBA_DH + 1, :] for h in range(MOBA_HEADS)], axis=0)
    o_ref[0] = out_t.T.astype(o_ref.dtype)


def _moba(q, k, v):
    b, t, _ = q.shape
    blk = MOBA_BLOCK
    nb = t // blk
    slopes, kx, qx, causal, ucol = _moba_consts()
    kern = functools.partial(_moba_kernel, nb=nb, slopes=tuple(float(s) for s in slopes))
    full = pl.BlockSpec((1, t, D_MIX), lambda bi, i: (bi, 0, 0))
    return pl.pallas_call(
        kern,
        grid=(b, nb),
        in_specs=[pl.BlockSpec((1, blk, D_MIX), lambda bi, i: (bi, i, 0)), full, full,
                  _const_spec(kx.shape), _const_spec(qx.shape), _const_spec(causal.shape),
                  _const_spec(ucol.shape)],
        out_specs=pl.BlockSpec((1, blk, D_MIX), lambda bi, i: (bi, i, 0)),
        out_shape=jax.ShapeDtypeStruct((b, t, D_MIX), BF16),
        scratch_shapes=[pltpu.VMEM((nb, D_MIX), F32),
                        pltpu.VMEM((MOBA_HEADS, _MOBA_VROWS, t), BF16),
                        pltpu.VMEM((MOBA_HEADS, _MOBA_VROWS, blk), F32),
                        pltpu.VMEM((MOBA_HEADS, blk), F32),
                        pltpu.VMEM((MOBA_HEADS, nb, blk), F32),
                        pltpu.VMEM((MOBA_HEADS, blk, 2 * LANES), BF16),
                        pltpu.VMEM((2, MOBA_HEADS, blk, blk), F32),
                        pltpu.VMEM((2, MOBA_HEADS, blk), F32)],
        compiler_params=_params("arbitrary", "arbitrary"),
        name="moba",
    )(q, k, v, jnp.asarray(kx, BF16), jnp.asarray(qx, BF16), jnp.asarray(causal), jnp.asarray(ucol))


_POOL_SUB = 128


def _pool_bands():
    r = np.arange(_POOL_SUB)[:, None] + _POOL_SUB
    c = np.arange(2 * _POOL_SUB)[None, :]
    return np.stack([((r - c >= 0) & (r - c < w)).astype(np.float32) for w in POOL_WINDOWS])


def _mix_kernel(x_ref, oa_ref, oc_ref, pu_ref, halo_ref, gates_ref, band_ref, pw_ref, ps_ref,
                wa_ref, wb_ref, wc_ref, wo_ref, nw_ref, o_ref, *, seq_len):
    tm = x_ref.shape[0]
    sub = _POOL_SUB
    t0 = (pl.program_id(0) * tm) % seq_len
    halo = halo_ref[...]
    halo = jnp.where(t0 > 0, halo, jnp.zeros_like(halo))
    ext = jnp.concatenate([halo, pu_ref[...]], axis=0)
    ya = _dot(oa_ref[...], wa_ref[...])
    yc = _dot(oc_ref[...], wc_ref[...])

    n_sub = tm // sub
    sums = [[_dot(band_ref[g], ext[j * sub:(j + 2) * sub, g * POOL_GC:(g + 1) * POOL_GC])
             for g in range(len(POOL_WINDOWS))] for j in range(n_sub)]
    tpos = t0 + lax.broadcasted_iota(jnp.int32, (tm, 1), 0)
    yg = []
    for g, w in enumerate(POOL_WINDOWS):
        s = jnp.concatenate([sums[j][g] for j in range(n_sub)], axis=0)
        cnt = jnp.minimum(tpos + 1, w).astype(F32)
        p = s / cnt - pu_ref[:, g * POOL_GC:(g + 1) * POOL_GC].astype(F32)
        yg.append(_dot(p.astype(BF16), pw_ref[g]))
    y = (jnp.concatenate(yg, axis=1) * ps_ref[...]).astype(BF16)
    yb = _dot(y, wb_ref[...])
    d = D_MODEL
    mixed = (_sigmoid(gates_ref[:, 0:d].astype(F32)) * ya
             + _sigmoid(gates_ref[:, d:2 * d].astype(F32)) * yb
             + _sigmoid(gates_ref[:, 2 * d:3 * d].astype(F32)) * yc)
    m2 = _dot(mixed.astype(BF16), wo_ref[...])
    o_ref[...] = x_ref[...] + _rms(m2, nw_ref[...])


def _mix(x, oa, oc, pu, gates, pool_w, pool_scale, wa, wb, wc, wo, nw, seq_len):
    n = x.shape[0]
    tm = _ROW_TILE
    sub = _POOL_SUB
    bands = jnp.asarray(_pool_bands(), BF16)
    row = lambda w: pl.BlockSpec((tm, w), lambda i: (i, 0))
    halo = pl.BlockSpec((sub, D_MIX), lambda i: (jnp.maximum(i * (tm // sub) - 1, 0), 0))
    return pl.pallas_call(
        functools.partial(_mix_kernel, seq_len=seq_len),
        grid=(n // tm,),
        in_specs=[row(D_MODEL), row(D_MIX), row(D_MIX), row(D_MIX), halo, row(3 * D_MODEL),
                  _const_spec(bands.shape), _const_spec(pool_w.shape), _const_spec((1, D_MIX)),
                  _const_spec(wa.shape), _const_spec(wb.shape), _const_spec(wc.shape),
                  _const_spec(wo.shape), _const_spec((1, D_MODEL))],
        out_specs=row(D_MODEL),
        out_shape=jax.ShapeDtypeStruct((n, D_MODEL), F32),
        compiler_params=_params("arbitrary"),
        name="mix",
    )(x, oa, oc, pu, pu, gates, bands, pool_w, pool_scale.reshape(1, D_MIX),
      wa, wb, wc, wo, nw.reshape(1, D_MODEL))


def _ffn_kernel(x_ref, npre_ref, wg_ref, wu_ref, wd_ref, npost_ref, o_ref):
    x = x_ref[...]
    h = _rms(x, npre_ref[...]).astype(BF16)
    g = _dot(h, wg_ref[...])
    u = _dot(h, wu_ref[...])
    a = (g * _sigmoid(g) * u).astype(BF16)
    f = _dot(a, wd_ref[...])
    o_ref[...] = x + _rms(f, npost_ref[...])


def _ffn(x, npre, wg, wu, wd, npost):
    n = x.shape[0]
    tm = _ROW_TILE
    row = pl.BlockSpec((tm, D_MODEL), lambda i: (i, 0))
    return pl.pallas_call(
        _ffn_kernel,
        grid=(n // tm,),
        in_specs=[row, _const_spec((1, D_MODEL)), _const_spec(wg.shape), _const_spec(wu.shape),
                  _const_spec(wd.shape), _const_spec((1, D_MODEL))],
        out_specs=row,
        out_shape=jax.ShapeDtypeStruct((n, D_MODEL), F32),
        compiler_params=_params("arbitrary"),
        name="ffn",
    )(x, npre.reshape(1, D_MODEL), wg, wu, wd, npost.reshape(1, D_MODEL))


def _prep_w_in(w):
    sizes = (256, 256, 512, GLA_RANK, 512, 512, 512, 512, 512, 3 * D_MODEL)
    offs = np.concatenate([[0], np.cumsum(sizes)])
    gq, gk, gv, g1, gr, pu, mq, mk, mv, gates = [w[:, offs[j]:offs[j + 1]] for j in range(len(sizes))]
    g1p = jnp.pad(g1, ((0, 0), (0, LANES - GLA_RANK)))
    return jnp.concatenate([gq, gk, gv, gr, pu, mq, mk, mv, gates, g1p], axis=1).astype(BF16)


def kernel(x, norm_mix_pre, w_in, gla_w_g2, gla_b_g, gla_norm, pool_w, pool_scale, w_branch_a,
           w_branch_b, w_branch_c, w_out, norm_mix_post, norm_ffn_pre, ffn_w_gate, ffn_w_up,
           ffn_w_down, norm_ffn_post):
    b, t, d = x.shape
    n = b * t
    xf = x.reshape(n, d)
    seq = lambda a: a.reshape(b, t, a.shape[-1])
    for l in range(w_in.shape[0]):
        gqk, gv, gr, pu, mq, mk, mv, gates, g1 = _inproj(xf, norm_mix_pre[l], _prep_w_in(w_in[l]))
        wg2 = jnp.pad(gla_w_g2[l], ((0, LANES - GLA_RANK), (0, 0))).astype(BF16)
        oa = _gla(seq(gqk), seq(gv), seq(g1), seq(gr), wg2,
                  gla_b_g[l].reshape(1, -1), gla_norm[l].reshape(1, -1))
        oc = _moba(seq(mq), seq(mk), seq(mv))
        xf = _mix(xf, oa.reshape(n, D_MIX), oc.reshape(n, D_MIX), pu, gates,
                  pool_w[l].astype(BF16), pool_scale[l],
                  w_branch_a[l].astype(BF16), w_branch_b[l].astype(BF16),
                  w_branch_c[l].astype(BF16), w_out[l].astype(BF16), norm_mix_post[l], t)
        xf = _ffn(xf, norm_ffn_pre[l], ffn_w_gate[l].astype(BF16), ffn_w_up[l].astype(BF16),
                  ffn_w_down[l].astype(BF16), norm_ffn_post[l])
    return xf.reshape(b, t, d)
```

```python
import functools

import numpy as np
import jax
import jax.numpy as jnp
from jax import lax
from jax.experimental import pallas as pl
from jax.experimental.pallas import tpu as pltpu

F32 = jnp.float32
BF16 = jnp.bfloat16

D_MODEL = 1024
D_MIX = 512
GLA_HEADS = 4
GLA_DK = 64
GLA_DV = 128
GLA_RANK = 16
GLA_TAU = 16.0
GLA_CHUNK = 64
POOL_WINDOWS = (2, 4, 8, 16)
POOL_GC = 128
MOBA_HEADS = 8
MOBA_DH = 64
MOBA_BLOCK = 256
MOBA_TOPK = 3
D_FF = 2816
EPS = 1e-6
NEG = -1e30

LANES = 128
VMEM_LIMIT = 56 * 1024 * 1024

_IN_WIDTHS = (512, 512, 512, 512, 512, 512, 512, 3072)
LOG2E = 1.4426950408889634
_IN_SCALES = (1.0, 1.0, 1.0, 1.0, MOBA_DH ** -0.5 * LOG2E, 1.0, 1.0, 0.5)
_ROW_TILE = 512
_GLA_FAST_MAX_DECAY = 40.0


def _dot(a, b):
    return jnp.dot(a, b, preferred_element_type=F32)


def _dot_nt(a, b):
    return lax.dot_general(a, b, (((1,), (1,)), ((), ())), preferred_element_type=F32)


def _dot_tn(a, b):
    return lax.dot_general(a, b, (((0,), (0,)), ((), ())), preferred_element_type=F32)


def _rms(x, w):
    ms = jnp.mean(x * x, axis=-1, keepdims=True)
    return x * lax.rsqrt(ms + EPS) * w


def _sigmoid(x):
    return 1.0 / (1.0 + jnp.exp(-x))


def _params(*sem):
    return pltpu.CompilerParams(dimension_semantics=sem, vmem_limit_bytes=VMEM_LIMIT)


def _const_spec(shape):
    nd = len(shape)
    return pl.BlockSpec(shape, lambda *_: (0,) * nd, pipeline_mode=pl.Buffered(1))


def _layer_spec(stacked, layer):
    nd = stacked.ndim - 1
    return pl.BlockSpec((None,) + stacked.shape[1:], lambda *_: (layer,) + (0,) * nd,
                        pipeline_mode=pl.Buffered(1))


def _inproj_kernel(x_ref, nw_ref, w_ref, wg2_ref, bg_ref, *o_refs):
    *group_refs, la_ref, lamax_ref = o_refs

    @pl.when(pl.program_id(0) == 0)
    def _():
        lamax_ref[...] = jnp.zeros_like(lamax_ref)

    h = _rms(x_ref[...], nw_ref[...]).astype(BF16)
    g1 = _dot(h, w_ref[:, sum(_IN_WIDTHS):])
    logit = _dot(g1.astype(BF16), wg2_ref[...]) + bg_ref[...]
    log_a = (jnp.minimum(logit, 0.0) - jnp.log(1.0 + jnp.exp(-jnp.abs(logit)))) * (1.0 / GLA_TAU)
    la_ref[...] = log_a
    col_max = jnp.max(-log_a, axis=0, keepdims=True)
    lamax_ref[...] = jnp.maximum(lamax_ref[...],
                                 jnp.maximum(col_max[:, :LANES], col_max[:, LANES:]))
    off = 0
    for o_ref, scale in zip(group_refs, _IN_SCALES):
        width = o_ref.shape[-1]
        for c0 in range(0, width, 512):
            cw = min(512, width - c0)
            acc = _dot(h, w_ref[:, off + c0:off + c0 + cw])
            if scale != 1.0:
                acc = acc * scale
            o_ref[:, c0:c0 + cw] = acc.astype(o_ref.dtype)
        off += width


def _inproj(x, nw, wcat, wg2, bg, layer):
    n = x.shape[0]
    tm = _ROW_TILE
    gate_w = GLA_HEADS * GLA_DK
    row = lambda w: pl.BlockSpec((tm, w), lambda i: (i, 0))
    return pl.pallas_call(
        _inproj_kernel,
        grid=(n // tm,),
        in_specs=[row(D_MODEL), _layer_spec(nw, layer), _layer_spec(wcat, layer),
                  _layer_spec(wg2, layer), _layer_spec(bg, layer)],
        out_specs=[row(w) for w in _IN_WIDTHS]
        + [row(gate_w), pl.BlockSpec((8, LANES), lambda i: (0, 0))],
        out_shape=[jax.ShapeDtypeStruct((n, w), BF16) for w in _IN_WIDTHS]
        + [jax.ShapeDtypeStruct((n, gate_w), F32), jax.ShapeDtypeStruct((8, LANES), F32)],
        compiler_params=_params("arbitrary"),
        name="inproj",
    )(x, nw, wcat, wg2, bg)


_GLA_HALVES = (32, 16, 8, 4, 2, 1)


def _gla_consts():
    c = GLA_CHUNK
    t = np.arange(c)
    tri = (t[None, :] <= t[:, None]).astype(np.float32)
    blocks = [tri]
    masks = []
    for m in _GLA_HALVES:
        ref = (t // (2 * m)) * (2 * m) + m - 1
        blocks.append(tri[ref])
        same = (t[:, None] // (2 * m)) == (t[None, :] // (2 * m))
        upper = (t[:, None] % (2 * m)) >= m
        lower = (t[None, :] % (2 * m)) < m
        masks.append((same & upper & lower).astype(np.float32))
    blocks.append(tri[np.full(c, c - 1)])
    masks.append(np.eye(c, dtype=np.float32))
    sel = np.concatenate(blocks, 0)
    return np.concatenate([sel, sel], 1), np.stack(masks)


def _gla_consts_fast():
    c = GLA_CHUNK
    t = np.arange(c)
    tri = (t[None, :] <= t[:, None]).astype(np.float32)
    sel = np.concatenate([tri, tri[np.full(c, c - 1)]], 0)
    return np.concatenate([sel, sel], 1), tri[None]


def _gla_kernel(qk_ref, v_ref, la_ref, r_ref, nw_ref, sel_ref, mask_ref,
                o_ref, st_ref, *, n_chunks, fast):
    c = GLA_CHUNK
    nlev = 0 if fast else len(_GLA_HALVES)

    @pl.when(pl.program_id(1) == 0)
    def _():
        st_ref[...] = jnp.zeros_like(st_ref)

    lane = lax.broadcasted_iota(jnp.int32, (c, LANES), 1)
    first_half = lane < GLA_DK
    lane_st = lax.broadcasted_iota(jnp.int32, (GLA_DV, LANES), 1) < GLA_DK

    chunks = range(n_chunks)
    heads = range(GLA_HEADS)
    keep = [first_half if h % 2 == 0 else jnp.logical_not(first_half) for h in heads]

    def pair(x, h):
        return x[:, (h // 2) * LANES:(h // 2 + 1) * LANES]

    def masked(x, h):
        xp = pair(x, h)
        return jnp.where(keep[h], xp, jnp.zeros_like(xp))

    qk = qk_ref[0]
    q_all = qk[:, :256].astype(F32) * (GLA_DK ** -0.5)
    k_all = qk[:, 256:].astype(F32)
    log_a = la_ref[0]
    la_hi = log_a.astype(BF16)
    la_lo = (log_a - la_hi.astype(F32)).astype(BF16)
    rows = [slice(ci * c, (ci + 1) * c) for ci in chunks]
    q = [q_all[r] for r in rows]
    k = [k_all[r] for r in rows]
    v = [v_ref[0, r, :] for r in rows]
    rb = [_dot(sel_ref[...], jnp.concatenate([la_hi[r], la_lo[r]], axis=0)) for r in rows]
    bc = [x[0:c] for x in rb]
    b_last = [x[(nlev + 1) * c:(nlev + 2) * c] for x in rb]
    q_dec = [(q[ci] * jnp.exp(bc[ci])).astype(BF16) for ci in chunks]
    k_dec = [(k[ci] * jnp.exp(b_last[ci] - bc[ci])).astype(BF16) for ci in chunks]
    st_decay = [jnp.exp(b_last[ci][0:1, :]) for ci in chunks]

    if fast:
        causal = mask_ref[0] > 0.5
        k_grow = [(k[ci] * jnp.exp(-bc[ci])).astype(BF16) for ci in chunks]
        a = [[jnp.where(causal, _dot_nt(masked(q_dec[ci], h), pair(k_grow[ci], h)), 0.0)
              for h in heads] for ci in chunks]
    else:
        a = [[jnp.zeros((c, c), F32) for _ in heads] for _ in chunks]
        for lv in range(nlev + 1):
            msk = mask_ref[lv]
            for ci in chunks:
                if lv < nlev:
                    w = jnp.exp(-jnp.abs(bc[ci] - rb[ci][(lv + 1) * c:(lv + 2) * c]))
                    qs = (q[ci] * w).astype(BF16)
                    ks = (k[ci] * w).astype(BF16)
                else:
                    qs = q[ci].astype(BF16)
                    ks = k[ci].astype(BF16)
                for h in heads:
                    a[ci][h] = a[ci][h] + _dot_nt(masked(qs, h), pair(ks, h)) * msk

    v_h = [[v[ci][:, h * GLA_DV:(h + 1) * GLA_DV] for h in heads] for ci in chunks]
    o_intra = [[_dot(a[ci][h].astype(BF16), v_h[ci][h]) for h in heads] for ci in chunks]
    upd = [[_dot_tn(v_h[ci][h], pair(k_dec[ci], h)) for h in heads] for ci in chunks]

    st = st_ref[...]
    for ci in chunks:
        st_b = st.astype(BF16)
        for h in heads:
            o = o_intra[ci][h] + _dot_nt(masked(q_dec[ci], h), pair(st_b, h))
            ms = jnp.mean(o * o, axis=-1, keepdims=True)
            on = o * lax.rsqrt(ms + EPS) * nw_ref[...]
            rg = r_ref[0, rows[ci], h * GLA_DV:(h + 1) * GLA_DV].astype(F32)
            o_ref[0, rows[ci], h * GLA_DV:(h + 1) * GLA_DV] = (on * (rg * _sigmoid(rg))).astype(o_ref.dtype)
        st = st_decay[ci] * st + jnp.concatenate(
            [jnp.where(lane_st, upd[ci][2 * pr], upd[ci][2 * pr + 1]) for pr in range(GLA_HEADS // 2)],
            axis=1)
    st_ref[...] = st


def _gla(qk, v, la, r, nw, layer, fast, tg=512):
    b, t, _ = qk.shape
    sel_np, mask_np = _gla_consts_fast() if fast else _gla_consts()
    sel = jnp.asarray(sel_np, BF16)
    masks = jnp.asarray(mask_np, F32)
    kern = functools.partial(_gla_kernel, n_chunks=tg // GLA_CHUNK, fast=fast)
    seq = lambda w: pl.BlockSpec((1, tg, w), lambda bi, i: (bi, i, 0))
    return pl.pallas_call(
        kern,
        grid=(b, t // tg),
        in_specs=[seq(512), seq(512), seq(GLA_HEADS * GLA_DK), seq(512),
                  _layer_spec(nw, layer), _const_spec(sel.shape), _const_spec(masks.shape)],
        out_specs=seq(512),
        out_shape=jax.ShapeDtypeStruct((b, t, D_MIX), BF16),
        scratch_shapes=[pltpu.VMEM((GLA_DV, GLA_HEADS * GLA_DK), F32)],
        compiler_params=_params("arbitrary", "arbitrary"),
        name="gla_fast" if fast else "gla",
    )(qk, v, la, r, nw, sel, masks)


def _moba_consts():
    blk = MOBA_BLOCK
    c = (2.0 ** (-8.0 * (np.arange(MOBA_HEADS) + 1) / MOBA_HEADS) * LOG2E).astype(np.float32)
    pos = np.arange(blk, dtype=np.float32)
    kx = np.zeros((blk, LANES), np.float32)
    kx[:, 0:3] = pos[:, None]
    pieces = []
    rest = c.copy()
    for _ in range(3):
        piece = rest.astype(jnp.bfloat16).astype(np.float32)
        pieces.append(piece)
        rest = rest - piece
    qx = np.zeros((MOBA_HEADS, blk, LANES), np.float32)
    for j, piece in enumerate(pieces):
        qx[:, :, j] = piece[:, None]
    causal = np.where(pos[:, None] <= pos[None, :], 0.0, NEG).astype(np.float32)
    ucol = c[:, None] * pos[None, :]
    return c, kx, qx, causal, ucol.astype(np.float32)


_MOBA_VROWS = MOBA_DH + 16


def _moba_kernel(q_ref, k_ref, v_ref, kx_ref, qx_ref, causal_ref, ucol_ref, o_ref,
                 kmean_ref, vt_ref, acc_ref, m_ref, sel_ref, qa_ref, z_ref, bmax_ref,
                 *, nb, slopes):
    blk = MOBA_BLOCK
    i = pl.program_id(1)

    @pl.when(i == 0)
    def _():
        for n in range(nb):
            kb = k_ref[0, n * blk:(n + 1) * blk, :].astype(F32)
            kmean_ref[n:n + 1, :] = jnp.sum(kb, axis=0, keepdims=True) * (1.0 / blk)
            vbt = v_ref[0, n * blk:(n + 1) * blk, :].astype(F32).T
            for h in range(MOBA_HEADS):
                vt_ref[h, 0:MOBA_DH, n * blk:(n + 1) * blk] = (
                    vbt[h * MOBA_DH:(h + 1) * MOBA_DH, :].astype(BF16))
                vt_ref[h, MOBA_DH:_MOBA_VROWS, n * blk:(n + 1) * blk] = jnp.ones(
                    (_MOBA_VROWS - MOBA_DH, blk), BF16)

    lane = lax.broadcasted_iota(jnp.int32, (blk, LANES), 1)
    first_half = lane < MOBA_DH
    rowid = lax.broadcasted_iota(jnp.int32, (nb, blk), 0)

    for h in range(MOBA_HEADS):
        pr = h // 2
        qp = q_ref[0, :, pr * LANES:(pr + 1) * LANES]
        keep = first_half if h % 2 == 0 else jnp.logical_not(first_half)
        qm = jnp.where(keep, qp, jnp.zeros_like(qp))
        qa_ref[h, :, 0:LANES] = qm
        qa_ref[h, :, LANES:2 * LANES] = qx_ref[h]
        km = kmean_ref[:, pr * LANES:(pr + 1) * LANES]
        km_hi = km.astype(BF16)
        km_lo = (km - km_hi.astype(F32)).astype(BF16)
        g = _dot_nt(km_hi, qm) + _dot_nt(km_lo, qm)
        g = jnp.where(rowid < i, g, NEG)
        sel = jnp.where(rowid == i, 1.0, 0.0)
        for r in range(MOBA_TOPK):
            mx = jnp.max(g, axis=0, keepdims=True)
            idx = jnp.min(jnp.where(g == mx, rowid, nb), axis=0, keepdims=True)
            pick = rowid == idx
            sel = jnp.where(jnp.logical_and(pick, i > r), 1.0, sel)
            g = jnp.where(pick, -3.0e38, g)
        sel_ref[h] = sel

    acc_ref[...] = jnp.zeros_like(acc_ref)
    m_ref[...] = jnp.full_like(m_ref, NEG)

    def scores(h, n, slot, own):
        pr = h // 2
        kp = k_ref[0, pl.ds(pl.multiple_of(n * blk, blk), blk), pr * LANES:(pr + 1) * LANES]
        z = _dot_nt(jnp.concatenate([kp, kx_ref[...]], axis=1), qa_ref[h])
        if own:
            z = z + causal_ref[...]
        z_ref[slot, h] = z
        bmax_ref[slot, h:h + 1, :] = jnp.max(z, axis=0, keepdims=True)

    def accumulate(h, n, slot):
        u = ucol_ref[h:h + 1, :] + (slopes[h] * blk) * (i - n).astype(F32)
        picked = sel_ref[h, pl.ds(n, 1), :] > 0.5
        m_old = m_ref[h:h + 1, :]
        m_new = jnp.where(picked, jnp.maximum(m_old, bmax_ref[slot, h:h + 1, :] - u), m_old)
        off = jnp.where(picked, m_new + u, -NEG)
        p = jnp.exp2(z_ref[slot, h] - off)
        alpha = jnp.exp2(m_old - m_new)
        m_ref[h:h + 1, :] = m_new
        vt = vt_ref[h, :, pl.ds(pl.multiple_of(n * blk, blk), blk)]
        acc_ref[h] = alpha * acc_ref[h] + _dot(vt, p.astype(BF16))

    def step(score_args, acc_args):
        for h in range(MOBA_HEADS):
            if score_args is not None:
                scores(h, *score_args)
            if acc_args is not None:
                accumulate(h, *acc_args)

    @pl.when(i == 0)
    def _():
        step((0, 0, True), None)
        step(None, (0, 0))

    @pl.when(i > 0)
    def _():
        step((0, 0, False), None)

    def two_steps(j, carry):
        n1 = 2 * j + 1
        step((n1, 1, False), (n1 - 1, 0))
        step((n1 + 1, 0, False), (n1, 1))
        return carry

    lax.fori_loop(0, (i - 1) // 2, two_steps, 0)

    @pl.when(i % 2 == 1)
    def _():
        step((i, 1, True), (i - 1, 0))
        step(None, (i, 1))

    @pl.when(jnp.logical_and(i % 2 == 0, i > 0))
    def _():
        step((i - 1, 1, False), (i - 2, 0))
        step((i, 0, True), (i - 1, 1))
        step(None, (i, 0))

    out_t = jnp.concatenate(
        [acc_ref[h, 0:MOBA_DH, :] / acc_ref[h, MOBA_DH:MOBA_DH + 1, :] for h in range(MOBA_HEADS)], axis=0)
    o_ref[0] = out_t.T.astype(o_ref.dtype)


def _moba(q, k, v):
    b, t, _ = q.shape
    blk = MOBA_BLOCK
    nb = t // blk
    slopes, kx, qx, causal, ucol = _moba_consts()
    kern = functools.partial(_moba_kernel, nb=nb, slopes=tuple(float(s) for s in slopes))
    full = pl.BlockSpec((1, t, D_MIX), lambda bi, i: (bi, 0, 0))
    return pl.pallas_call(
        kern,
        grid=(b, nb),
        in_specs=[pl.BlockSpec((1, blk, D_MIX), lambda bi, i: (bi, i, 0)), full, full,
                  _const_spec(kx.shape), _const_spec(qx.shape), _const_spec(causal.shape),
                  _const_spec(ucol.shape)],
        out_specs=pl.BlockSpec((1, blk, D_MIX), lambda bi, i: (bi, i, 0)),
        out_shape=jax.ShapeDtypeStruct((b, t, D_MIX), BF16),
        scratch_shapes=[pltpu.VMEM((nb, D_MIX), F32),
                        pltpu.VMEM((MOBA_HEADS, _MOBA_VROWS, t), BF16),
                        pltpu.VMEM((MOBA_HEADS, _MOBA_VROWS, blk), F32),
                        pltpu.VMEM((MOBA_HEADS, blk), F32),
                        pltpu.VMEM((MOBA_HEADS, nb, blk), F32),
                        pltpu.VMEM((MOBA_HEADS, blk, 2 * LANES), BF16),
                        pltpu.VMEM((2, MOBA_HEADS, blk, blk), F32),
                        pltpu.VMEM((2, MOBA_HEADS, blk), F32)],
        compiler_params=_params("arbitrary", "arbitrary"),
        name="moba",
    )(q, k, v, jnp.asarray(kx, BF16), jnp.asarray(qx, BF16), jnp.asarray(causal), jnp.asarray(ucol))


_POOL_SUB = 128


def _pool_bands():
    r = np.arange(_POOL_SUB)[:, None] + _POOL_SUB
    c = np.arange(2 * _POOL_SUB)[None, :]
    return np.stack([((r - c >= 0) & (r - c < w)).astype(np.float32) for w in POOL_WINDOWS])


def _mix_kernel(x_ref, oa_ref, oc_ref, pu_ref, halo_ref, gates_ref, band_ref, pw_ref, ps_ref,
                wa_ref, wb_ref, wc_ref, wo_ref, nw_ref, o_ref, *, seq_len):
    tm = x_ref.shape[0]
    sub = _POOL_SUB
    t0 = (pl.program_id(0) * tm) % seq_len
    halo = halo_ref[...]
    halo = jnp.where(t0 > 0, halo, jnp.zeros_like(halo))
    ext = jnp.concatenate([halo, pu_ref[...]], axis=0)
    n_sub = tm // sub
    sums = [[_dot(band_ref[g], ext[j * sub:(j + 2) * sub, g * POOL_GC:(g + 1) * POOL_GC])
             for g in range(len(POOL_WINDOWS))] for j in range(n_sub)]
    tpos = t0 + lax.broadcasted_iota(jnp.int32, (tm, 1), 0)
    yg = []
    for g, w in enumerate(POOL_WINDOWS):
        s = jnp.concatenate([sums[j][g] for j in range(n_sub)], axis=0)
        cnt = jnp.minimum(tpos + 1, w).astype(F32)
        p = s / cnt - pu_ref[:, g * POOL_GC:(g + 1) * POOL_GC].astype(F32)
        yg.append(_dot(p.astype(BF16), pw_ref[g]))
    y = (jnp.concatenate(yg, axis=1) * ps_ref[...]).astype(BF16)

    ya = _dot(oa_ref[...], wa_ref[...])
    yb = _dot(y, wb_ref[...])
    yc = _dot(oc_ref[...], wc_ref[...])
    d = D_MODEL
    mixed = 0.5 * (jnp.tanh(gates_ref[:, 0:d].astype(F32)) * ya
                   + jnp.tanh(gates_ref[:, d:2 * d].astype(F32)) * yb
                   + jnp.tanh(gates_ref[:, 2 * d:3 * d].astype(F32)) * yc
                   + (ya + yb + yc))
    m2 = _dot(mixed.astype(BF16), wo_ref[...])
    o_ref[...] = x_ref[...] + _rms(m2, nw_ref[...])


def _mix(x, oa, oc, pu, gates, pool_w, pool_scale, wa, wb, wc, wo, nw, layer, seq_len):
    n = x.shape[0]
    tm = _ROW_TILE
    sub = _POOL_SUB
    bands = jnp.asarray(_pool_bands(), BF16)
    row = lambda w: pl.BlockSpec((tm, w), lambda i: (i, 0))
    halo = pl.BlockSpec((sub, D_MIX), lambda i: (jnp.maximum(i * (tm // sub) - 1, 0), 0))
    params = (pool_w, pool_scale, wa, wb, wc, wo, nw)
    return pl.pallas_call(
        functools.partial(_mix_kernel, seq_len=seq_len),
        grid=(n // tm,),
        in_specs=[row(D_MODEL), row(D_MIX), row(D_MIX), row(D_MIX), halo, row(3 * D_MODEL),
                  _const_spec(bands.shape)] + [_layer_spec(p, layer) for p in params],
        out_specs=row(D_MODEL),
        out_shape=jax.ShapeDtypeStruct((n, D_MODEL), F32),
        compiler_params=_params("arbitrary"),
        name="mix",
    )(x, oa, oc, pu, pu, gates, bands, *params)


def _ffn_kernel(x_ref, npre_ref, wg_ref, wu_ref, wd_ref, npost_ref, o_ref):
    x = x_ref[...]
    h = _rms(x, npre_ref[...]).astype(BF16)
    g = _dot(h, wg_ref[...])
    u = _dot(h, wu_ref[...])
    a = (g * _sigmoid(g) * u).astype(BF16)
    f = _dot(a, wd_ref[...])
    o_ref[...] = x + _rms(f, npost_ref[...])


def _ffn(x, npre, wg, wu, wd, npost, layer):
    n = x.shape[0]
    tm = _ROW_TILE
    row = pl.BlockSpec((tm, D_MODEL), lambda i: (i, 0))
    params = (npre, wg, wu, wd, npost)
    return pl.pallas_call(
        _ffn_kernel,
        grid=(n // tm,),
        in_specs=[row] + [_layer_spec(p, layer) for p in params],
        out_specs=row,
        out_shape=jax.ShapeDtypeStruct((n, D_MODEL), F32),
        compiler_params=_params("arbitrary"),
        name="ffn",
    )(x, *params)


def _prep_w_in(w):
    sizes = (256, 256, 512, GLA_RANK, 512, 512, 512, 512, 512, 3 * D_MODEL)
    offs = np.concatenate([[0], np.cumsum(sizes)])
    gq, gk, gv, g1, gr, pu, mq, mk, mv, gates = [w[..., offs[j]:offs[j + 1]] for j in range(len(sizes))]
    g1p = jnp.pad(g1, ((0, 0), (0, 0), (0, LANES - GLA_RANK)))
    return jnp.concatenate([gq, gk, gv, gr, pu, mq, mk, mv, gates, g1p], axis=-1).astype(BF16)


def kernel(x, norm_mix_pre, w_in, gla_w_g2, gla_b_g, gla_norm, pool_w, pool_scale, w_branch_a,
           w_branch_b, w_branch_c, w_out, norm_mix_post, norm_ffn_pre, ffn_w_gate, ffn_w_up,
           ffn_w_down, norm_ffn_post):
    b, t, d = x.shape
    n = b * t
    xf = x.reshape(n, d)
    seq = lambda a: a.reshape(b, t, a.shape[-1])
    row = lambda a: a[:, None, :]
    bf = lambda a: a.astype(BF16)
    w_cat = _prep_w_in(w_in)
    wg2 = bf(jnp.pad(gla_w_g2, ((0, 0), (0, LANES - GLA_RANK), (0, 0))))
    mix_params = (bf(pool_w), row(pool_scale), bf(w_branch_a), bf(w_branch_b), bf(w_branch_c),
                  bf(w_out), row(norm_mix_post))
    ffn_params = (row(norm_ffn_pre), bf(ffn_w_gate), bf(ffn_w_up), bf(ffn_w_down), row(norm_ffn_post))
    gla_nw = row(gla_norm)
    for l in range(w_in.shape[0]):
        gqk, gv, gr, pu, mq, mk, mv, gates, la, la_max = _inproj(
            xf, row(norm_mix_pre), w_cat, wg2, row(gla_b_g), l)
        gla_args = (seq(gqk), seq(gv), seq(la), seq(gr), gla_nw)
        bounded = jnp.max(la_max) * (GLA_CHUNK - 1) <= _GLA_FAST_MAX_DECAY
        oa = lax.cond(bounded,
                      functools.partial(_gla, layer=l, fast=True),
                      functools.partial(_gla, layer=l, fast=False), *gla_args)
        oc = _moba(seq(mq), seq(mk), seq(mv))
        xf = _mix(xf, oa.reshape(n, D_MIX), oc.reshape(n, D_MIX), pu, gates, *mix_params, l, t)
        xf = _ffn(xf, *ffn_params, l)
    return xf.reshape(b, t, d)
```

```python
import functools

import numpy as np
import jax
import jax.numpy as jnp
from jax import lax
from jax.experimental import pallas as pl
from jax.experimental.pallas import tpu as pltpu

F32 = jnp.float32
BF16 = jnp.bfloat16

D_MODEL = 1024
D_MIX = 512
GLA_HEADS = 4
GLA_DK = 64
GLA_DV = 128
GLA_RANK = 16
GLA_TAU = 16.0
GLA_CHUNK = 64
POOL_WINDOWS = (2, 4, 8, 16)
POOL_GC = 128
MOBA_HEADS = 8
MOBA_DH = 64
MOBA_BLOCK = 256
MOBA_TOPK = 3
D_FF = 2816
EPS = 1e-6
NEG = -1e30

LANES = 128
VMEM_LIMIT = 56 * 1024 * 1024

_IN_WIDTHS = (512, 512, 512, 512, 512, 512, 512, 3072)
LOG2E = 1.4426950408889634
_IN_SCALES = (1.0, 1.0, 1.0, 1.0, MOBA_DH ** -0.5 * LOG2E, 1.0, 1.0, 0.5)
_ROW_TILE = 512
_GLA_FAST_MAX_DECAY = 40.0


def _dot(a, b):
    return jnp.dot(a, b, preferred_element_type=F32)


def _dot_nt(a, b):
    return lax.dot_general(a, b, (((1,), (1,)), ((), ())), preferred_element_type=F32)


def _dot_tn(a, b):
    return lax.dot_general(a, b, (((0,), (0,)), ((), ())), preferred_element_type=F32)


def _rms(x, w):
    ms = jnp.mean(x * x, axis=-1, keepdims=True)
    return x * lax.rsqrt(ms + EPS) * w


def _sigmoid(x):
    return 1.0 / (1.0 + jnp.exp(-x))


def _params(*sem):
    return pltpu.CompilerParams(dimension_semantics=sem, vmem_limit_bytes=VMEM_LIMIT)


def _const_spec(shape):
    nd = len(shape)
    return pl.BlockSpec(shape, lambda *_: (0,) * nd, pipeline_mode=pl.Buffered(1))


def _layer_spec(stacked, layer):
    nd = stacked.ndim - 1
    return pl.BlockSpec((None,) + stacked.shape[1:], lambda *_: (layer,) + (0,) * nd,
                        pipeline_mode=pl.Buffered(1))


def _inproj_kernel(x_ref, nw_ref, w_ref, wg2_ref, bg_ref, *o_refs):
    *group_refs, la_ref, lamax_ref = o_refs

    @pl.when(pl.program_id(0) == 0)
    def _():
        lamax_ref[...] = jnp.zeros_like(lamax_ref)

    h = _rms(x_ref[...], nw_ref[...]).astype(BF16)
    g1 = _dot(h, w_ref[:, sum(_IN_WIDTHS):])
    logit = _dot(g1.astype(BF16), wg2_ref[...]) + bg_ref[...]
    log_a = (jnp.minimum(logit, 0.0) - jnp.log(1.0 + jnp.exp(-jnp.abs(logit)))) * (1.0 / GLA_TAU)
    la_ref[...] = log_a
    col_max = jnp.max(-log_a, axis=0, keepdims=True)
    lamax_ref[...] = jnp.maximum(lamax_ref[...],
                                 jnp.maximum(col_max[:, :LANES], col_max[:, LANES:]))
    off = 0
    for o_ref, scale in zip(group_refs, _IN_SCALES):
        width = o_ref.shape[-1]
        for c0 in range(0, width, 512):
            cw = min(512, width - c0)
            acc = _dot(h, w_ref[:, off + c0:off + c0 + cw])
            if scale != 1.0:
                acc = acc * scale
            o_ref[:, c0:c0 + cw] = acc.astype(o_ref.dtype)
        off += width


def _inproj(x, nw, wcat, wg2, bg, layer):
    n = x.shape[0]
    tm = _ROW_TILE
    gate_w = GLA_HEADS * GLA_DK
    row = lambda w: pl.BlockSpec((tm, w), lambda i: (i, 0))
    return pl.pallas_call(
        _inproj_kernel,
        grid=(n // tm,),
        in_specs=[row(D_MODEL), _layer_spec(nw, layer), _layer_spec(wcat, layer),
                  _layer_spec(wg2, layer), _layer_spec(bg, layer)],
        out_specs=[row(w) for w in _IN_WIDTHS]
        + [row(gate_w), pl.BlockSpec((8, LANES), lambda i: (0, 0))],
        out_shape=[jax.ShapeDtypeStruct((n, w), BF16) for w in _IN_WIDTHS]
        + [jax.ShapeDtypeStruct((n, gate_w), F32), jax.ShapeDtypeStruct((8, LANES), F32)],
        compiler_params=_params("arbitrary"),
        name="inproj",
    )(x, nw, wcat, wg2, bg)


_GLA_HALVES = (32, 16, 8, 4, 2, 1)


def _gla_consts():
    c = GLA_CHUNK
    t = np.arange(c)
    tri = (t[None, :] <= t[:, None]).astype(np.float32)
    blocks = [tri]
    masks = []
    for m in _GLA_HALVES:
        ref = (t // (2 * m)) * (2 * m) + m - 1
        blocks.append(tri[ref])
        same = (t[:, None] // (2 * m)) == (t[None, :] // (2 * m))
        upper = (t[:, None] % (2 * m)) >= m
        lower = (t[None, :] % (2 * m)) < m
        masks.append((same & upper & lower).astype(np.float32))
    blocks.append(tri[np.full(c, c - 1)])
    masks.append(np.eye(c, dtype=np.float32))
    sel = np.concatenate(blocks, 0)
    return np.concatenate([sel, sel], 1), np.stack(masks)


def _gla_consts_fast():
    c = GLA_CHUNK
    t = np.arange(c)
    tri = (t[None, :] <= t[:, None]).astype(np.float32)
    sel = np.concatenate([tri, tri[np.full(c, c - 1)]], 0)
    return np.concatenate([sel, sel], 1), tri[None]


def _gla_kernel(qk_ref, v_ref, la_ref, r_ref, nw_ref, sel_ref, mask_ref,
                o_ref, st_ref, *, n_chunks, fast):
    c = GLA_CHUNK
    nlev = 0 if fast else len(_GLA_HALVES)

    @pl.when(pl.program_id(1) == 0)
    def _():
        st_ref[...] = jnp.zeros_like(st_ref)

    lane = lax.broadcasted_iota(jnp.int32, (c, LANES), 1)
    first_half = lane < GLA_DK
    lane_st = lax.broadcasted_iota(jnp.int32, (GLA_DV, LANES), 1) < GLA_DK

    chunks = range(n_chunks)
    heads = range(GLA_HEADS)
    keep = [first_half if h % 2 == 0 else jnp.logical_not(first_half) for h in heads]

    def pair(x, h):
        return x[:, (h // 2) * LANES:(h // 2 + 1) * LANES]

    def masked(x, h):
        xp = pair(x, h)
        return jnp.where(keep[h], xp, jnp.zeros_like(xp))

    qk = qk_ref[0]
    q_all = qk[:, :256].astype(F32) * (GLA_DK ** -0.5)
    k_all = qk[:, 256:].astype(F32)
    log_a = la_ref[0]
    la_hi = log_a.astype(BF16)
    la_lo = (log_a - la_hi.astype(F32)).astype(BF16)
    rows = [slice(ci * c, (ci + 1) * c) for ci in chunks]
    q = [q_all[r] for r in rows]
    k = [k_all[r] for r in rows]
    v = [v_ref[0, r, :] for r in rows]
    rb = [_dot(sel_ref[...], jnp.concatenate([la_hi[r], la_lo[r]], axis=0)) for r in rows]
    bc = [x[0:c] for x in rb]
    b_last = [x[(nlev + 1) * c:(nlev + 2) * c] for x in rb]
    q_dec = [(q[ci] * jnp.exp(bc[ci])).astype(BF16) for ci in chunks]
    k_dec = [(k[ci] * jnp.exp(b_last[ci] - bc[ci])).astype(BF16) for ci in chunks]
    st_decay = [jnp.exp(b_last[ci][0:1, :]) for ci in chunks]

    if fast:
        causal = mask_ref[0] > 0.5
        k_grow = [(k[ci] * jnp.exp(-bc[ci])).astype(BF16) for ci in chunks]
        a = [[jnp.where(causal, _dot_nt(masked(q_dec[ci], h), pair(k_grow[ci], h)), 0.0)
              for h in heads] for ci in chunks]
    else:
        a = [[jnp.zeros((c, c), F32) for _ in heads] for _ in chunks]
        for lv in range(nlev + 1):
            msk = mask_ref[lv]
            for ci in chunks:
                if lv < nlev:
                    w = jnp.exp(-jnp.abs(bc[ci] - rb[ci][(lv + 1) * c:(lv + 2) * c]))
                    qs = (q[ci] * w).astype(BF16)
                    ks = (k[ci] * w).astype(BF16)
                else:
                    qs = q[ci].astype(BF16)
                    ks = k[ci].astype(BF16)
                for h in heads:
                    a[ci][h] = a[ci][h] + _dot_nt(masked(qs, h), pair(ks, h)) * msk

    v_h = [[v[ci][:, h * GLA_DV:(h + 1) * GLA_DV] for h in heads] for ci in chunks]
    o_intra = [[_dot(a[ci][h].astype(BF16), v_h[ci][h]) for h in heads] for ci in chunks]
    upd = [[_dot_tn(v_h[ci][h], pair(k_dec[ci], h)) for h in heads] for ci in chunks]

    st = st_ref[...]
    for ci in chunks:
        st_b = st.astype(BF16)
        for h in heads:
            o = o_intra[ci][h] + _dot_nt(masked(q_dec[ci], h), pair(st_b, h))
            ms = jnp.mean(o * o, axis=-1, keepdims=True)
            on = o * lax.rsqrt(ms + EPS) * nw_ref[...]
            rg = r_ref[0, rows[ci], h * GLA_DV:(h + 1) * GLA_DV].astype(F32)
            o_ref[0, rows[ci], h * GLA_DV:(h + 1) * GLA_DV] = (on * (rg * _sigmoid(rg))).astype(o_ref.dtype)
        st = st_decay[ci] * st + jnp.concatenate(
            [jnp.where(lane_st, upd[ci][2 * pr], upd[ci][2 * pr + 1]) for pr in range(GLA_HEADS // 2)],
            axis=1)
    st_ref[...] = st


def _gla(qk, v, la, r, nw, layer, fast, tg=512):
    b, t, _ = qk.shape
    sel_np, mask_np = _gla_consts_fast() if fast else _gla_consts()
    sel = jnp.asarray(sel_np, BF16)
    masks = jnp.asarray(mask_np, F32)
    kern = functools.partial(_gla_kernel, n_chunks=tg // GLA_CHUNK, fast=fast)
    seq = lambda w: pl.BlockSpec((1, tg, w), lambda bi, i: (bi, i, 0))
    return pl.pallas_call(
        kern,
        grid=(b, t // tg),
        in_specs=[seq(512), seq(512), seq(GLA_HEADS * GLA_DK), seq(512),
                  _layer_spec(nw, layer), _const_spec(sel.shape), _const_spec(masks.shape)],
        out_specs=seq(512),
        out_shape=jax.ShapeDtypeStruct((b, t, D_MIX), BF16),
        scratch_shapes=[pltpu.VMEM((GLA_DV, GLA_HEADS * GLA_DK), F32)],
        compiler_params=_params("arbitrary", "arbitrary"),
        name="gla_fast" if fast else "gla",
    )(qk, v, la, r, nw, sel, masks)


def _moba_consts():
    blk = MOBA_BLOCK
    c = (2.0 ** (-8.0 * (np.arange(MOBA_HEADS) + 1) / MOBA_HEADS) * LOG2E).astype(np.float32)
    pos = np.arange(blk, dtype=np.float32)
    kx = np.zeros((blk, LANES), np.float32)
    kx[:, 0:3] = pos[:, None]
    pieces = []
    rest = c.copy()
    for _ in range(3):
        piece = rest.astype(jnp.bfloat16).astype(np.float32)
        pieces.append(piece)
        rest = rest - piece
    qx = np.zeros((MOBA_HEADS, LANES, blk), np.float32)
    for j, piece in enumerate(pieces):
        qx[:, j, :] = piece[:, None]
    causal = np.where(pos[:, None] <= pos[None, :], 0.0, NEG).astype(np.float32)
    causal = np.stack([np.zeros_like(causal), causal])
    ucol = c[:, None] * pos[None, :]
    return c, kx, qx, causal, ucol.astype(np.float32)


_MOBA_VROWS = MOBA_DH + 16


def _moba_kernel(q_ref, k_ref, v_ref, kx_ref, qx_ref, causal_ref, ucol_ref, o_ref,
                 kmean_ref, vt_ref, acc_ref, m_ref, sel_ref, qa_ref, z_ref, bmax_ref,
                 *, nb, slopes):
    blk = MOBA_BLOCK
    i = pl.program_id(1)

    @pl.when(i == 0)
    def _():
        for n in range(nb):
            kb = k_ref[0, n * blk:(n + 1) * blk, :].astype(F32)
            kmean_ref[n:n + 1, :] = jnp.sum(kb, axis=0, keepdims=True) * (1.0 / blk)
            vbt = v_ref[0, n * blk:(n + 1) * blk, :].astype(F32).T
            for h in range(MOBA_HEADS):
                vt_ref[h, 0:MOBA_DH, n * blk:(n + 1) * blk] = (
                    vbt[h * MOBA_DH:(h + 1) * MOBA_DH, :].astype(BF16))
                vt_ref[h, MOBA_DH:_MOBA_VROWS, n * blk:(n + 1) * blk] = jnp.ones(
                    (_MOBA_VROWS - MOBA_DH, blk), BF16)

    lane = lax.broadcasted_iota(jnp.int32, (blk, LANES), 1)
    first_half = lane < MOBA_DH
    rowid = lax.broadcasted_iota(jnp.int32, (nb, blk), 0)

    acc_ref[...] = jnp.zeros_like(acc_ref)
    m_ref[...] = jnp.full_like(m_ref, NEG)

    def scores(h, n, slot, own):
        pr = h // 2
        kp = k_ref[0, pl.ds(pl.multiple_of(n * blk, blk), blk), pr * LANES:(pr + 1) * LANES]
        z = _dot(jnp.concatenate([kp, kx_ref[...]], axis=1), qa_ref[h])
        if not (isinstance(own, int) and own == 0):
            z = z + causal_ref[own]
        z_ref[slot, h] = z
        bmax_ref[slot, h:h + 1, :] = jnp.max(z, axis=0, keepdims=True)

    for h in range(MOBA_HEADS):
        pr = h // 2
        qp = q_ref[0, :, pr * LANES:(pr + 1) * LANES]
        keep = first_half if h % 2 == 0 else jnp.logical_not(first_half)
        qm = jnp.where(keep, qp, jnp.zeros_like(qp))
        qmt = qm.astype(F32).T.astype(BF16)
        qa_ref[h, 0:LANES, :] = qmt
        qa_ref[h, LANES:2 * LANES, :] = qx_ref[h]
        km = kmean_ref[:, pr * LANES:(pr + 1) * LANES]
        km_hi = km.astype(BF16)
        km_lo = (km - km_hi.astype(F32)).astype(BF16)
        g = _dot(km_hi, qmt) + _dot(km_lo, qmt)
        g = jnp.where(rowid < i, g, NEG)
        sel = jnp.where(rowid == i, 1.0, 0.0)
        for r in range(MOBA_TOPK):
            mx = jnp.max(g, axis=0, keepdims=True)
            idx = jnp.min(jnp.where(g == mx, rowid, nb), axis=0, keepdims=True)
            pick = rowid == idx
            sel = jnp.where(jnp.logical_and(pick, i > r), 1.0, sel)
            g = jnp.where(pick, -3.0e38, g)
        sel_ref[h] = sel
        scores(h, 0, 0, (i == 0).astype(jnp.int32))

    def accumulate(h, n, slot):
        u = ucol_ref[h:h + 1, :] + (slopes[h] * blk) * (i - n).astype(F32)
        picked = sel_ref[h, pl.ds(n, 1), :] > 0.5
        m_old = m_ref[h:h + 1, :]
        m_new = jnp.where(picked, jnp.maximum(m_old, bmax_ref[slot, h:h + 1, :] - u), m_old)
        off = jnp.where(picked, m_new + u, -NEG)
        p = jnp.exp2(z_ref[slot, h] - off)
        alpha = jnp.exp2(m_old - m_new)
        m_ref[h:h + 1, :] = m_new
        vt = vt_ref[h, :, pl.ds(pl.multiple_of(n * blk, blk), blk)]
        acc_ref[h] = alpha * acc_ref[h] + _dot(vt, p.astype(BF16))

    def step(score_args, acc_args):
        for h in range(MOBA_HEADS):
            if score_args is not None:
                scores(h, *score_args)
            if acc_args is not None:
                accumulate(h, *acc_args)

    @pl.when(i == 0)
    def _():
        step(None, (0, 0))

    def two_steps(j, carry):
        n1 = 2 * j + 1
        step((n1, 1, 0), (n1 - 1, 0))
        step((n1 + 1, 0, 0), (n1, 1))
        return carry

    lax.fori_loop(0, (i - 1) // 2, two_steps, 0)

    @pl.when(i % 2 == 1)
    def _():
        step((i, 1, 1), (i - 1, 0))
        step(None, (i, 1))

    @pl.when(jnp.logical_and(i % 2 == 0, i > 0))
    def _():
        step((i - 1, 1, 0), (i - 2, 0))
        step((i, 0, 1), (i - 1, 1))
        step(None, (i, 0))

    out_t = jnp.concatenate(
        [acc_ref[h, 0:MOBA_DH, :] / acc_ref[h, MOBA_DH:MOBA_DH + 1, :] for h in range(MOBA_HEADS)], axis=0)
    o_ref[0] = out_t.T.astype(o_ref.dtype)


def _moba(q, k, v):
    b, t, _ = q.shape
    blk = MOBA_BLOCK
    nb = t // blk
    slopes, kx, qx, causal, ucol = _moba_consts()
    kern = functools.partial(_moba_kernel, nb=nb, slopes=tuple(float(s) for s in slopes))
    full = pl.BlockSpec((1, t, D_MIX), lambda bi, i: (bi, 0, 0))
    return pl.pallas_call(
        kern,
        grid=(b, nb),
        in_specs=[pl.BlockSpec((1, blk, D_MIX), lambda bi, i: (bi, i, 0)), full, full,
                  _const_spec(kx.shape), _const_spec(qx.shape), _const_spec(causal.shape),
                  _const_spec(ucol.shape)],
        out_specs=pl.BlockSpec((1, blk, D_MIX), lambda bi, i: (bi, i, 0)),
        out_shape=jax.ShapeDtypeStruct((b, t, D_MIX), BF16),
        scratch_shapes=[pltpu.VMEM((nb, D_MIX), F32),
                        pltpu.VMEM((MOBA_HEADS, _MOBA_VROWS, t), BF16),
                        pltpu.VMEM((MOBA_HEADS, _MOBA_VROWS, blk), F32),
                        pltpu.VMEM((MOBA_HEADS, blk), F32),
                        pltpu.VMEM((MOBA_HEADS, nb, blk), F32),
                        pltpu.VMEM((MOBA_HEADS, 2 * LANES, blk), BF16),
                        pltpu.VMEM((2, MOBA_HEADS, blk, blk), F32),
                        pltpu.VMEM((2, MOBA_HEADS, blk), F32)],
        compiler_params=_params("arbitrary", "arbitrary"),
        name="moba",
    )(q, k, v, jnp.asarray(kx, BF16), jnp.asarray(qx, BF16), jnp.asarray(causal), jnp.asarray(ucol))


_POOL_SUB = 128


def _pool_bands():
    r = np.arange(_POOL_SUB)[:, None] + _POOL_SUB
    c = np.arange(2 * _POOL_SUB)[None, :]
    return np.stack([((r - c >= 0) & (r - c < w)).astype(np.float32) for w in POOL_WINDOWS])


def _mix_kernel(x_ref, oa_ref, oc_ref, pu_ref, halo_ref, gates_ref, band_ref, pw_ref, ps_ref,
                wa_ref, wb_ref, wc_ref, wo_ref, nw_ref, o_ref, *, seq_len):
    tm = x_ref.shape[0]
    sub = _POOL_SUB
    t0 = (pl.program_id(0) * tm) % seq_len
    halo = halo_ref[...]
    halo = jnp.where(t0 > 0, halo, jnp.zeros_like(halo))
    ext = jnp.concatenate([halo, pu_ref[...]], axis=0)
    n_sub = tm // sub
    sums = [[_dot(band_ref[g], ext[j * sub:(j + 2) * sub, g * POOL_GC:(g + 1) * POOL_GC])
             for g in range(len(POOL_WINDOWS))] for j in range(n_sub)]
    tpos = t0 + lax.broadcasted_iota(jnp.int32, (tm, 1), 0)
    yg = []
    for g, w in enumerate(POOL_WINDOWS):
        s = jnp.concatenate([sums[j][g] for j in range(n_sub)], axis=0)
        cnt = jnp.minimum(tpos + 1, w).astype(F32)
        p = s / cnt - pu_ref[:, g * POOL_GC:(g + 1) * POOL_GC].astype(F32)
        yg.append(_dot(p.astype(BF16), pw_ref[g]))
    y = (jnp.concatenate(yg, axis=1) * ps_ref[...]).astype(BF16)

    ya = _dot(oa_ref[...], wa_ref[...])
    yb = _dot(y, wb_ref[...])
    yc = _dot(oc_ref[...], wc_ref[...])
    d = D_MODEL
    mixed = 0.5 * (jnp.tanh(gates_ref[:, 0:d].astype(F32)) * ya
                   + jnp.tanh(gates_ref[:, d:2 * d].astype(F32)) * yb
                   + jnp.tanh(gates_ref[:, 2 * d:3 * d].astype(F32)) * yc
                   + (ya + yb + yc))
    m2 = _dot(mixed.astype(BF16), wo_ref[...])
    o_ref[...] = x_ref[...] + _rms(m2, nw_ref[...])


def _mix(x, oa, oc, pu, gates, pool_w, pool_scale, wa, wb, wc, wo, nw, layer, seq_len):
    n = x.shape[0]
    tm = _ROW_TILE
    sub = _POOL_SUB
    bands = jnp.asarray(_pool_bands(), BF16)
    row = lambda w: pl.BlockSpec((tm, w), lambda i: (i, 0))
    halo = pl.BlockSpec((sub, D_MIX), lambda i: (jnp.maximum(i * (tm // sub) - 1, 0), 0))
    params = (pool_w, pool_scale, wa, wb, wc, wo, nw)
    return pl.pallas_call(
        functools.partial(_mix_kernel, seq_len=seq_len),
        grid=(n // tm,),
        in_specs=[row(D_MODEL), row(D_MIX), row(D_MIX), row(D_MIX), halo, row(3 * D_MODEL),
                  _const_spec(bands.shape)] + [_layer_spec(p, layer) for p in params],
        out_specs=row(D_MODEL),
        out_shape=jax.ShapeDtypeStruct((n, D_MODEL), F32),
        compiler_params=_params("arbitrary"),
        name="mix",
    )(x, oa, oc, pu, pu, gates, bands, *params)


def _ffn_kernel(x_ref, npre_ref, wg_ref, wu_ref, wd_ref, npost_ref, o_ref):
    x = x_ref[...]
    h = _rms(x, npre_ref[...]).astype(BF16)
    g = _dot(h, wg_ref[...])
    u = _dot(h, wu_ref[...])
    a = (g * _sigmoid(g) * u).astype(BF16)
    f = _dot(a, wd_ref[...])
    o_ref[...] = x + _rms(f, npost_ref[...])


def _ffn(x, npre, wg, wu, wd, npost, layer):
    n = x.shape[0]
    tm = _ROW_TILE
    row = pl.BlockSpec((tm, D_MODEL), lambda i: (i, 0))
    params = (npre, wg, wu, wd, npost)
    return pl.pallas_call(
        _ffn_kernel,
        grid=(n // tm,),
        in_specs=[row] + [_layer_spec(p, layer) for p in params],
        out_specs=row,
        out_shape=jax.ShapeDtypeStruct((n, D_MODEL), F32),
        compiler_params=_params("arbitrary"),
        name="ffn",
    )(x, *params)


def _prep_w_in(w):
    g1_lo = 2 * GLA_HEADS * GLA_DK + GLA_HEADS * GLA_DV
    g1_hi = g1_lo + GLA_RANK
    pad = jnp.zeros(w.shape[:-1] + (LANES - GLA_RANK,), w.dtype)
    return jnp.concatenate([w[..., :g1_lo], w[..., g1_hi:], w[..., g1_lo:g1_hi], pad],
                           axis=-1).astype(BF16)


def kernel(x, norm_mix_pre, w_in, gla_w_g2, gla_b_g, gla_norm, pool_w, pool_scale, w_branch_a,
           w_branch_b, w_branch_c, w_out, norm_mix_post, norm_ffn_pre, ffn_w_gate, ffn_w_up,
           ffn_w_down, norm_ffn_post):
    b, t, d = x.shape
    n = b * t
    xf = x.reshape(n, d)
    seq = lambda a: a.reshape(b, t, a.shape[-1])
    row = lambda a: a[:, None, :]
    bf = lambda a: a.astype(BF16)
    w_cat = _prep_w_in(w_in)
    wg2 = bf(jnp.pad(gla_w_g2, ((0, 0), (0, LANES - GLA_RANK), (0, 0))))
    mix_params = (bf(pool_w), row(pool_scale), bf(w_branch_a), bf(w_branch_b), bf(w_branch_c),
                  bf(w_out), row(norm_mix_post))
    ffn_params = (row(norm_ffn_pre), bf(ffn_w_gate), bf(ffn_w_up), bf(ffn_w_down), row(norm_ffn_post))
    gla_nw = row(gla_norm)
    for l in range(w_in.shape[0]):
        gqk, gv, gr, pu, mq, mk, mv, gates, la, la_max = _inproj(
            xf, row(norm_mix_pre), w_cat, wg2, row(gla_b_g), l)
        gla_args = (seq(gqk), seq(gv), seq(la), seq(gr), gla_nw)
        bounded = jnp.max(la_max) * (GLA_CHUNK - 1) <= _GLA_FAST_MAX_DECAY
        oa = lax.cond(bounded,
                      functools.partial(_gla, layer=l, fast=True),
                      functools.partial(_gla, layer=l, fast=False), *gla_args)
        oc = _moba(seq(mq), seq(mk), seq(mv))
        xf = _mix(xf, oa.reshape(n, D_MIX), oc.reshape(n, D_MIX), pu, gates, *mix_params, l, t)
        xf = _ffn(xf, *ffn_params, l)
    return xf.reshape(b, t, d)
```

```python
import functools

import numpy as np
import jax
import jax.numpy as jnp
from jax import lax
from jax.experimental import pallas as pl
from jax.experimental.pallas import tpu as pltpu

F32 = jnp.float32
BF16 = jnp.bfloat16

D_MODEL = 1024
D_MIX = 512
GLA_HEADS = 4
GLA_DK = 64
GLA_DV = 128
GLA_RANK = 16
GLA_TAU = 16.0
GLA_CHUNK = 64
POOL_WINDOWS = (2, 4, 8, 16)
POOL_GC = 128
MOBA_HEADS = 8
MOBA_DH = 64
MOBA_BLOCK = 256
MOBA_TOPK = 3
D_FF = 2816
EPS = 1e-6
NEG = -1e30

LANES = 128
VMEM_LIMIT = 56 * 1024 * 1024

_IN_WIDTHS = (512, 512, 512, 512, 512, 512, 512, 3072)
LOG2E = 1.4426950408889634
_IN_SCALES = (1.0, 1.0, 1.0, 1.0, MOBA_DH ** -0.5 * LOG2E, 1.0, 1.0, 0.5)
_ROW_TILE = 512
_GLA_FAST_MAX_DECAY = 40.0


def _dot(a, b):
    return jnp.dot(a, b, preferred_element_type=F32)


def _dot_nt(a, b):
    return lax.dot_general(a, b, (((1,), (1,)), ((), ())), preferred_element_type=F32)


def _dot_tn(a, b):
    return lax.dot_general(a, b, (((0,), (0,)), ((), ())), preferred_element_type=F32)


def _rms(x, w):
    ms = jnp.mean(x * x, axis=-1, keepdims=True)
    return x * lax.rsqrt(ms + EPS) * w


def _sigmoid(x):
    return 1.0 / (1.0 + jnp.exp(-x))


def _params(*sem):
    return pltpu.CompilerParams(dimension_semantics=sem, vmem_limit_bytes=VMEM_LIMIT)


def _const_spec(shape):
    nd = len(shape)
    return pl.BlockSpec(shape, lambda *_: (0,) * nd, pipeline_mode=pl.Buffered(1))


def _layer_spec(stacked, layer):
    nd = stacked.ndim - 1
    return pl.BlockSpec((None,) + stacked.shape[1:], lambda *_: (layer,) + (0,) * nd,
                        pipeline_mode=pl.Buffered(1))


def _inproj_kernel(x_ref, nw_ref, w_ref, wg2_ref, bg_ref, *o_refs):
    *group_refs, la_ref, lamax_ref = o_refs

    @pl.when(pl.program_id(0) == 0)
    def _():
        lamax_ref[...] = jnp.zeros_like(lamax_ref)

    h = _rms(x_ref[...], nw_ref[...]).astype(BF16)
    g1 = _dot(h, w_ref[:, sum(_IN_WIDTHS):])
    logit = _dot(g1.astype(BF16), wg2_ref[...]) + bg_ref[...]
    log_a = (jnp.minimum(logit, 0.0) - jnp.log(1.0 + jnp.exp(-jnp.abs(logit)))) * (1.0 / GLA_TAU)
    la_ref[...] = log_a
    col_max = jnp.max(-log_a, axis=0, keepdims=True)
    lamax_ref[...] = jnp.maximum(lamax_ref[...],
                                 jnp.maximum(col_max[:, :LANES], col_max[:, LANES:]))
    off = 0
    for o_ref, scale in zip(group_refs, _IN_SCALES):
        width = o_ref.shape[-1]
        for c0 in range(0, width, 512):
            cw = min(512, width - c0)
            acc = _dot(h, w_ref[:, off + c0:off + c0 + cw])
            if scale != 1.0:
                acc = acc * scale
            o_ref[:, c0:c0 + cw] = acc.astype(o_ref.dtype)
        off += width


def _inproj(x, nw, wcat, wg2, bg, layer):
    n = x.shape[0]
    tm = _ROW_TILE
    gate_w = GLA_HEADS * GLA_DK
    row = lambda w: pl.BlockSpec((tm, w), lambda i: (i, 0))
    return pl.pallas_call(
        _inproj_kernel,
        grid=(n // tm,),
        in_specs=[row(D_MODEL), _layer_spec(nw, layer), _layer_spec(wcat, layer),
                  _layer_spec(wg2, layer), _layer_spec(bg, layer)],
        out_specs=[row(w) for w in _IN_WIDTHS]
        + [row(gate_w), pl.BlockSpec((8, LANES), lambda i: (0, 0))],
        out_shape=[jax.ShapeDtypeStruct((n, w), BF16) for w in _IN_WIDTHS]
        + [jax.ShapeDtypeStruct((n, gate_w), F32), jax.ShapeDtypeStruct((8, LANES), F32)],
        compiler_params=_params("arbitrary"),
        name="inproj",
    )(x, nw, wcat, wg2, bg)


_GLA_HALVES = (32, 16, 8, 4, 2, 1)


def _gla_consts():
    c = GLA_CHUNK
    t = np.arange(c)
    tri = (t[None, :] <= t[:, None]).astype(np.float32)
    blocks = [tri]
    masks = []
    for m in _GLA_HALVES:
        ref = (t // (2 * m)) * (2 * m) + m - 1
        blocks.append(tri[ref])
        same = (t[:, None] // (2 * m)) == (t[None, :] // (2 * m))
        upper = (t[:, None] % (2 * m)) >= m
        lower = (t[None, :] % (2 * m)) < m
        masks.append((same & upper & lower).astype(np.float32))
    blocks.append(tri[np.full(c, c - 1)])
    masks.append(np.eye(c, dtype=np.float32))
    sel = np.concatenate(blocks, 0)
    return np.concatenate([sel, sel], 1), np.stack(masks)


def _gla_consts_fast():
    c = GLA_CHUNK
    t = np.arange(c)
    tri = (t[None, :] <= t[:, None]).astype(np.float32)
    sel = np.concatenate([tri, tri[np.full(c, c - 1)]], 0)
    return np.concatenate([sel, sel], 1), tri[None]


def _gla_kernel(qk_ref, v_ref, la_ref, r_ref, nw_ref, sel_ref, mask_ref,
                o_ref, st_ref, *, n_chunks, fast):
    c = GLA_CHUNK
    nlev = 0 if fast else len(_GLA_HALVES)

    @pl.when(pl.program_id(1) == 0)
    def _():
        st_ref[...] = jnp.zeros_like(st_ref)

    lane = lax.broadcasted_iota(jnp.int32, (c, LANES), 1)
    first_half = lane < GLA_DK
    lane_st = lax.broadcasted_iota(jnp.int32, (GLA_DV, LANES), 1) < GLA_DK

    chunks = range(n_chunks)
    heads = range(GLA_HEADS)
    keep = [first_half if h % 2 == 0 else jnp.logical_not(first_half) for h in heads]

    def pair(x, h):
        return x[:, (h // 2) * LANES:(h // 2 + 1) * LANES]

    def masked(x, h):
        xp = pair(x, h)
        return jnp.where(keep[h], xp, jnp.zeros_like(xp))

    qk = qk_ref[0]
    q_all = qk[:, :256].astype(F32) * (GLA_DK ** -0.5)
    k_all = qk[:, 256:].astype(F32)
    log_a = la_ref[0]
    la_hi = log_a.astype(BF16)
    la_lo = (log_a - la_hi.astype(F32)).astype(BF16)
    rows = [slice(ci * c, (ci + 1) * c) for ci in chunks]
    q = [q_all[r] for r in rows]
    k = [k_all[r] for r in rows]
    v = [v_ref[0, r, :] for r in rows]
    rb = [_dot(sel_ref[...], jnp.concatenate([la_hi[r], la_lo[r]], axis=0)) for r in rows]
    bc = [x[0:c] for x in rb]
    b_last = [x[(nlev + 1) * c:(nlev + 2) * c] for x in rb]
    q_dec = [(q[ci] * jnp.exp(bc[ci])).astype(BF16) for ci in chunks]
    k_dec = [(k[ci] * jnp.exp(b_last[ci] - bc[ci])).astype(BF16) for ci in chunks]
    st_decay = [jnp.exp(b_last[ci][0:1, :]) for ci in chunks]

    if fast:
        causal = mask_ref[0] > 0.5
        k_grow = [(k[ci] * jnp.exp(-bc[ci])).astype(BF16) for ci in chunks]
        a = [[jnp.where(causal, _dot_nt(masked(q_dec[ci], h), pair(k_grow[ci], h)), 0.0)
              for h in heads] for ci in chunks]
    else:
        a = [[jnp.zeros((c, c), F32) for _ in heads] for _ in chunks]
        for lv in range(nlev + 1):
            msk = mask_ref[lv]
            for ci in chunks:
                if lv < nlev:
                    w = jnp.exp(-jnp.abs(bc[ci] - rb[ci][(lv + 1) * c:(lv + 2) * c]))
                    qs = (q[ci] * w).astype(BF16)
                    ks = (k[ci] * w).astype(BF16)
                else:
                    qs = q[ci].astype(BF16)
                    ks = k[ci].astype(BF16)
                for h in heads:
                    a[ci][h] = a[ci][h] + _dot_nt(masked(qs, h), pair(ks, h)) * msk

    v_h = [[v[ci][:, h * GLA_DV:(h + 1) * GLA_DV] for h in heads] for ci in chunks]
    o_intra = [[_dot(a[ci][h].astype(BF16), v_h[ci][h]) for h in heads] for ci in chunks]
    upd = [[_dot_tn(v_h[ci][h], pair(k_dec[ci], h)) for h in heads] for ci in chunks]

    st = st_ref[...]
    for ci in chunks:
        st_b = st.astype(BF16)
        for h in heads:
            o = o_intra[ci][h] + _dot_nt(masked(q_dec[ci], h), pair(st_b, h))
            ms = jnp.mean(o * o, axis=-1, keepdims=True)
            on = o * lax.rsqrt(ms + EPS) * nw_ref[...]
            rg = r_ref[0, rows[ci], h * GLA_DV:(h + 1) * GLA_DV].astype(F32)
            o_ref[0, rows[ci], h * GLA_DV:(h + 1) * GLA_DV] = (on * (rg * _sigmoid(rg))).astype(o_ref.dtype)
        st = st_decay[ci] * st + jnp.concatenate(
            [jnp.where(lane_st, upd[ci][2 * pr], upd[ci][2 * pr + 1]) for pr in range(GLA_HEADS // 2)],
            axis=1)
    st_ref[...] = st


def _gla(qk, v, la, r, nw, layer, fast, tg=512):
    b, t, _ = qk.shape
    sel_np, mask_np = _gla_consts_fast() if fast else _gla_consts()
    sel = jnp.asarray(sel_np, BF16)
    masks = jnp.asarray(mask_np, F32)
    kern = functools.partial(_gla_kernel, n_chunks=tg // GLA_CHUNK, fast=fast)
    seq = lambda w: pl.BlockSpec((1, tg, w), lambda bi, i: (bi, i, 0))
    return pl.pallas_call(
        kern,
        grid=(b, t // tg),
        in_specs=[seq(512), seq(512), seq(GLA_HEADS * GLA_DK), seq(512),
                  _layer_spec(nw, layer), _const_spec(sel.shape), _const_spec(masks.shape)],
        out_specs=seq(512),
        out_shape=jax.ShapeDtypeStruct((b, t, D_MIX), BF16),
        scratch_shapes=[pltpu.VMEM((GLA_DV, GLA_HEADS * GLA_DK), F32)],
        compiler_params=_params("arbitrary", "arbitrary"),
        name="gla_fast" if fast else "gla",
    )(qk, v, la, r, nw, sel, masks)


def _moba_consts():
    blk = MOBA_BLOCK
    c = (2.0 ** (-8.0 * (np.arange(MOBA_HEADS) + 1) / MOBA_HEADS) * LOG2E).astype(np.float32)
    pos = np.arange(blk, dtype=np.float32)
    kx = np.zeros((blk, LANES), np.float32)
    kx[:, 0:3] = pos[:, None]
    pieces = []
    rest = c.copy()
    for _ in range(3):
        piece = rest.astype(jnp.bfloat16).astype(np.float32)
        pieces.append(piece)
        rest = rest - piece
    qx = np.zeros((MOBA_HEADS, LANES, blk), np.float32)
    for j, piece in enumerate(pieces):
        qx[:, j, :] = piece[:, None]
    causal = np.where(pos[:, None] <= pos[None, :], 0.0, NEG).astype(np.float32)
    causal = np.stack([np.zeros_like(causal), causal])
    ucol = c[:, None] * pos[None, :]
    return c, kx, qx, causal, ucol.astype(np.float32)


_MOBA_VROWS = MOBA_DH + 16
_MOBA_UNROLL = 4


def _moba_kernel(q_ref, k_ref, v_ref, kx_ref, qx_ref, causal_ref, ucol_ref, o_ref,
                 kmean_ref, vt_ref, acc_ref, m_ref, sel_ref, qa_ref, z_ref, bmax_ref,
                 *, nb, slopes):
    blk = MOBA_BLOCK
    i = pl.program_id(1)

    @pl.when(i == 0)
    def _():
        for n in range(nb):
            kb = k_ref[0, n * blk:(n + 1) * blk, :].astype(F32)
            kmean_ref[n:n + 1, :] = jnp.sum(kb, axis=0, keepdims=True) * (1.0 / blk)
            vbt = v_ref[0, n * blk:(n + 1) * blk, :].astype(F32).T
            for h in range(MOBA_HEADS):
                vt_ref[h, 0:MOBA_DH, n * blk:(n + 1) * blk] = (
                    vbt[h * MOBA_DH:(h + 1) * MOBA_DH, :].astype(BF16))
                vt_ref[h, MOBA_DH:_MOBA_VROWS, n * blk:(n + 1) * blk] = jnp.ones(
                    (_MOBA_VROWS - MOBA_DH, blk), BF16)

    lane = lax.broadcasted_iota(jnp.int32, (blk, LANES), 1)
    first_half = lane < MOBA_DH
    rowid = lax.broadcasted_iota(jnp.int32, (nb, blk), 0)

    acc_ref[...] = jnp.zeros_like(acc_ref)
    m_ref[...] = jnp.full_like(m_ref, NEG)

    def scores(h, n, slot, own):
        pr = h // 2
        kp = k_ref[0, pl.ds(pl.multiple_of(n * blk, blk), blk), pr * LANES:(pr + 1) * LANES]
        z = _dot(jnp.concatenate([kp, kx_ref[...]], axis=1), qa_ref[h])
        if not (isinstance(own, int) and own == 0):
            z = z + causal_ref[own]
        z_ref[slot, h] = z
        bmax_ref[slot, h:h + 1, :] = jnp.max(z, axis=0, keepdims=True)

    for h in range(MOBA_HEADS):
        pr = h // 2
        qp = q_ref[0, :, pr * LANES:(pr + 1) * LANES]
        keep = first_half if h % 2 == 0 else jnp.logical_not(first_half)
        qm = jnp.where(keep, qp, jnp.zeros_like(qp))
        qmt = qm.astype(F32).T.astype(BF16)
        qa_ref[h, 0:LANES, :] = qmt
        qa_ref[h, LANES:2 * LANES, :] = qx_ref[h]
        km = kmean_ref[:, pr * LANES:(pr + 1) * LANES]
        km_hi = km.astype(BF16)
        km_lo = (km - km_hi.astype(F32)).astype(BF16)
        g = _dot(km_hi, qmt) + _dot(km_lo, qmt)
        g = jnp.where(rowid < i, g, NEG)
        sel = jnp.where(rowid == i, 1.0, 0.0)
        for r in range(MOBA_TOPK):
            mx = jnp.max(g, axis=0, keepdims=True)
            idx = jnp.min(jnp.where(g == mx, rowid, nb), axis=0, keepdims=True)
            pick = rowid == idx
            sel = jnp.where(jnp.logical_and(pick, i > r), 1.0, sel)
            g = jnp.where(pick, -3.0e38, g)
        sel_ref[h] = sel
        scores(h, 0, 0, (i == 0).astype(jnp.int32))

    def accumulate(h, n, slot):
        u = ucol_ref[h:h + 1, :] + (slopes[h] * blk) * (i - n).astype(F32)
        picked = sel_ref[h, pl.ds(n, 1), :] > 0.5
        m_old = m_ref[h:h + 1, :]
        m_new = jnp.where(picked, jnp.maximum(m_old, bmax_ref[slot, h:h + 1, :] - u), m_old)
        off = jnp.where(picked, m_new + u, -NEG)
        p = jnp.exp2(z_ref[slot, h] - off)
        alpha = jnp.exp2(m_old - m_new)
        m_ref[h:h + 1, :] = m_new
        vt = vt_ref[h, :, pl.ds(pl.multiple_of(n * blk, blk), blk)]
        acc_ref[h] = alpha * acc_ref[h] + _dot(vt, p.astype(BF16))

    def step(score_args, acc_args):
        for h in range(MOBA_HEADS):
            if score_args is not None:
                scores(h, *score_args)
            if acc_args is not None:
                accumulate(h, *acc_args)

    def regular_steps(k0, count):
        for k in range(count):
            step((k0 + k, (k + 1) % 2, 0), (k0 + k - 1, k % 2))

    @pl.when(i == 0)
    def _():
        step(None, (0, 0))

    def trip(j, carry):
        regular_steps(_MOBA_UNROLL * j + 1, _MOBA_UNROLL)
        return carry

    n_regular = jnp.maximum(i - 1, 0)
    n_trips = n_regular // _MOBA_UNROLL
    lax.fori_loop(0, n_trips, trip, 0)
    k1 = n_trips * _MOBA_UNROLL + 1
    for rem in range(_MOBA_UNROLL):
        @pl.when(jnp.logical_and(i > 0, n_regular % _MOBA_UNROLL == rem))
        def _(rem=rem):
            regular_steps(k1, rem)
            own_slot = (rem + 1) % 2
            step((i, own_slot, 1), (i - 1, 1 - own_slot))
            step(None, (i, own_slot))

    out_t = jnp.concatenate(
        [acc_ref[h, 0:MOBA_DH, :] / acc_ref[h, MOBA_DH:MOBA_DH + 1, :] for h in range(MOBA_HEADS)], axis=0)
    o_ref[0] = out_t.T.astype(o_ref.dtype)


def _moba(q, k, v):
    b, t, _ = q.shape
    blk = MOBA_BLOCK
    nb = t // blk
    slopes, kx, qx, causal, ucol = _moba_consts()
    kern = functools.partial(_moba_kernel, nb=nb, slopes=tuple(float(s) for s in slopes))
    full = pl.BlockSpec((1, t, D_MIX), lambda bi, i: (bi, 0, 0))
    return pl.pallas_call(
        kern,
        grid=(b, nb),
        in_specs=[pl.BlockSpec((1, blk, D_MIX), lambda bi, i: (bi, i, 0)), full, full,
                  _const_spec(kx.shape), _const_spec(qx.shape), _const_spec(causal.shape),
                  _const_spec(ucol.shape)],
        out_specs=pl.BlockSpec((1, blk, D_MIX), lambda bi, i: (bi, i, 0)),
        out_shape=jax.ShapeDtypeStruct((b, t, D_MIX), BF16),
        scratch_shapes=[pltpu.VMEM((nb, D_MIX), F32),
                        pltpu.VMEM((MOBA_HEADS, _MOBA_VROWS, t), BF16),
                        pltpu.VMEM((MOBA_HEADS, _MOBA_VROWS, blk), F32),
                        pltpu.VMEM((MOBA_HEADS, blk), F32),
                        pltpu.VMEM((MOBA_HEADS, nb, blk), F32),
                        pltpu.VMEM((MOBA_HEADS, 2 * LANES, blk), BF16),
                        pltpu.VMEM((2, MOBA_HEADS, blk, blk), F32),
                        pltpu.VMEM((2, MOBA_HEADS, blk), F32)],
        compiler_params=_params("arbitrary", "arbitrary"),
        name="moba",
    )(q, k, v, jnp.asarray(kx, BF16), jnp.asarray(qx, BF16), jnp.asarray(causal), jnp.asarray(ucol))


_POOL_SUB = 128


def _pool_bands():
    r = np.arange(_POOL_SUB)[:, None] + _POOL_SUB
    c = np.arange(2 * _POOL_SUB)[None, :]
    return np.stack([((r - c >= 0) & (r - c < w)).astype(np.float32) for w in POOL_WINDOWS])


def _mix_kernel(x_ref, oa_ref, oc_ref, pu_ref, halo_ref, gates_ref, band_ref, pw_ref, ps_ref,
                wa_ref, wb_ref, wc_ref, wo_ref, nw_ref, o_ref, *, seq_len):
    tm = x_ref.shape[0]
    sub = _POOL_SUB
    t0 = (pl.program_id(0) * tm) % seq_len
    halo = halo_ref[...]
    halo = jnp.where(t0 > 0, halo, jnp.zeros_like(halo))
    ext = jnp.concatenate([halo, pu_ref[...]], axis=0)
    n_sub = tm // sub
    sums = [[_dot(band_ref[g], ext[j * sub:(j + 2) * sub, g * POOL_GC:(g + 1) * POOL_GC])
             for g in range(len(POOL_WINDOWS))] for j in range(n_sub)]
    tpos = t0 + lax.broadcasted_iota(jnp.int32, (tm, 1), 0)
    yg = []
    for g, w in enumerate(POOL_WINDOWS):
        s = jnp.concatenate([sums[j][g] for j in range(n_sub)], axis=0)
        cnt = jnp.minimum(tpos + 1, w).astype(F32)
        p = s / cnt - pu_ref[:, g * POOL_GC:(g + 1) * POOL_GC].astype(F32)
        yg.append(_dot(p.astype(BF16), pw_ref[g]))
    y = (jnp.concatenate(yg, axis=1) * ps_ref[...]).astype(BF16)

    ya = _dot(oa_ref[...], wa_ref[...])
    yb = _dot(y, wb_ref[...])
    yc = _dot(oc_ref[...], wc_ref[...])
    d = D_MODEL
    mixed = 0.5 * (jnp.tanh(gates_ref[:, 0:d].astype(F32)) * ya
                   + jnp.tanh(gates_ref[:, d:2 * d].astype(F32)) * yb
                   + jnp.tanh(gates_ref[:, 2 * d:3 * d].astype(F32)) * yc
                   + (ya + yb + yc))
    m2 = _dot(mixed.astype(BF16), wo_ref[...])
    o_ref[...] = x_ref[...] + _rms(m2, nw_ref[...])


def _mix(x, oa, oc, pu, gates, pool_w, pool_scale, wa, wb, wc, wo, nw, layer, seq_len):
    n = x.shape[0]
    tm = _ROW_TILE
    sub = _POOL_SUB
    bands = jnp.asarray(_pool_bands(), BF16)
    row = lambda w: pl.BlockSpec((tm, w), lambda i: (i, 0))
    halo = pl.BlockSpec((sub, D_MIX), lambda i: (jnp.maximum(i * (tm // sub) - 1, 0), 0))
    params = (pool_w, pool_scale, wa, wb, wc, wo, nw)
    return pl.pallas_call(
        functools.partial(_mix_kernel, seq_len=seq_len),
        grid=(n // tm,),
        in_specs=[row(D_MODEL), row(D_MIX), row(D_MIX), row(D_MIX), halo, row(3 * D_MODEL),
                  _const_spec(bands.shape)] + [_layer_spec(p, layer) for p in params],
        out_specs=row(D_MODEL),
        out_shape=jax.ShapeDtypeStruct((n, D_MODEL), F32),
        compiler_params=_params("arbitrary"),
        name="mix",
    )(x, oa, oc, pu, pu, gates, bands, *params)


def _ffn_kernel(x_ref, npre_ref, wg_ref, wu_ref, wd_ref, npost_ref, o_ref):
    half = x_ref.shape[0] // 2
    for r in (slice(0, half), slice(half, 2 * half)):
        x = x_ref[r, :]
        h = _rms(x, npre_ref[...]).astype(BF16)
        g = _dot(h, wg_ref[...])
        u = _dot(h, wu_ref[...])
        a = (g * _sigmoid(g) * u).astype(BF16)
        f = _dot(a, wd_ref[...])
        o_ref[r, :] = x + _rms(f, npost_ref[...])


def _ffn(x, npre, wg, wu, wd, npost, layer):
    n = x.shape[0]
    tm = _ROW_TILE
    row = pl.BlockSpec((tm, D_MODEL), lambda i: (i, 0))
    params = (npre, wg, wu, wd, npost)
    return pl.pallas_call(
        _ffn_kernel,
        grid=(n // tm,),
        in_specs=[row] + [_layer_spec(p, layer) for p in params],
        out_specs=row,
        out_shape=jax.ShapeDtypeStruct((n, D_MODEL), F32),
        compiler_params=_params("arbitrary"),
        name="ffn",
    )(x, *params)


def _prep_w_in(w):
    sizes = (256, 256, 512, GLA_RANK, 512, 512, 512, 512, 512, 3 * D_MODEL)
    offs = np.concatenate([[0], np.cumsum(sizes)])
    gq, gk, gv, g1, gr, pu, mq, mk, mv, gates = [w[..., offs[j]:offs[j + 1]] for j in range(len(sizes))]
    g1p = jnp.pad(g1, ((0, 0), (0, 0), (0, LANES - GLA_RANK)))
    return jnp.concatenate([gq, gk, gv, gr, pu, mq, mk, mv, gates, g1p], axis=-1).astype(BF16)


def kernel(x, norm_mix_pre, w_in, gla_w_g2, gla_b_g, gla_norm, pool_w, pool_scale, w_branch_a,
           w_branch_b, w_branch_c, w_out, norm_mix_post, norm_ffn_pre, ffn_w_gate, ffn_w_up,
           ffn_w_down, norm_ffn_post):
    b, t, d = x.shape
    n = b * t
    xf = x.reshape(n, d)
    seq = lambda a: a.reshape(b, t, a.shape[-1])
    row = lambda a: a[:, None, :]
    bf = lambda a: a.astype(BF16)
    w_cat = _prep_w_in(w_in)
    wg2 = bf(jnp.pad(gla_w_g2, ((0, 0), (0, LANES - GLA_RANK), (0, 0))))
    mix_params = (bf(pool_w), row(pool_scale), bf(w_branch_a), bf(w_branch_b), bf(w_branch_c),
                  bf(w_out), row(norm_mix_post))
    ffn_params = (row(norm_ffn_pre), bf(ffn_w_gate), bf(ffn_w_up), bf(ffn_w_down), row(norm_ffn_post))
    gla_nw = row(gla_norm)
    for l in range(w_in.shape[0]):
        gqk, gv, gr, pu, mq, mk, mv, gates, la, la_max = _inproj(
            xf, row(norm_mix_pre), w_cat, wg2, row(gla_b_g), l)
        gla_args = (seq(gqk), seq(gv), seq(la), seq(gr), gla_nw)
        bounded = jnp.max(la_max) * (GLA_CHUNK - 1) <= _GLA_FAST_MAX_DECAY
        oa = lax.cond(bounded,
                      functools.partial(_gla, layer=l, fast=True),
                      functools.partial(_gla, layer=l, fast=False), *gla_args)
        oc = _moba(seq(mq), seq(mk), seq(mv))
        xf = _mix(xf, oa.reshape(n, D_MIX), oc.reshape(n, D_MIX), pu, gates, *mix_params, l, t)
        xf = _ffn(xf, *ffn_params, l)
    return xf.reshape(b, t, d)
```

```python
import functools
import math

import numpy as np
import jax
import jax.numpy as jnp
from jax import lax
from jax.experimental import pallas as pl
from jax.experimental.pallas import tpu as pltpu

F32 = jnp.float32
BF16 = jnp.bfloat16

D_MODEL = 1024
D_MIX = 512
GLA_HEADS = 4
GLA_DK = 64
GLA_DV = 128
GLA_RANK = 16
GLA_TAU = 16.0
GLA_CHUNK = 64
POOL_WINDOWS = (2, 4, 8, 16)
POOL_GC = 128
MOBA_HEADS = 8
MOBA_DH = 64
MOBA_BLOCK = 256
MOBA_TOPK = 3
D_FF = 2816
EPS = 1e-6
NEG = -1e30

LANES = 128
VMEM_LIMIT = 56 * 1024 * 1024

_IN_WIDTHS = (512, 512, 512, 512, 512, 512, 512, 3072)
LOG2E = 1.4426950408889634
_IN_SCALES = (1.0, 1.0, 1.0, 1.0, MOBA_DH ** -0.5 * LOG2E, 1.0, 1.0, 0.5)
_ROW_TILE = 512
_INPROJ_TILE = 1024
_FFN_TILE = 1024
_FFN_SUB = 256
_GLA_FAST_MAX_DECAY = 40.0


def _dot(a, b):
    return jnp.dot(a, b, preferred_element_type=F32)


def _dot_nt(a, b):
    return lax.dot_general(a, b, (((1,), (1,)), ((), ())), preferred_element_type=F32)


def _dot_tn(a, b):
    return lax.dot_general(a, b, (((0,), (0,)), ((), ())), preferred_element_type=F32)


def _rms(x, w):
    ms = jnp.mean(x * x, axis=-1, keepdims=True)
    return x * lax.rsqrt(ms + EPS) * w


def _sigmoid(x):
    return 1.0 / (1.0 + jnp.exp(-x))


def _params(*sem):
    return pltpu.CompilerParams(dimension_semantics=sem, vmem_limit_bytes=VMEM_LIMIT)


def _const_spec(shape):
    nd = len(shape)
    return pl.BlockSpec(shape, lambda *_: (0,) * nd, pipeline_mode=pl.Buffered(1))


def _layer_spec(stacked, layer):
    nd = stacked.ndim - 1
    return pl.BlockSpec((None,) + stacked.shape[1:], lambda *_: (layer,) + (0,) * nd,
                        pipeline_mode=pl.Buffered(1))


def _inproj_kernel(x_ref, nw_ref, w_ref, wg2_ref, bg_ref, *o_refs):
    *group_refs, la_ref, lamax_ref = o_refs

    @pl.when(pl.program_id(0) == 0)
    def _():
        lamax_ref[...] = jnp.zeros_like(lamax_ref)

    h = _rms(x_ref[...], nw_ref[...]).astype(BF16)
    g1 = _dot(h, w_ref[:, sum(_IN_WIDTHS):])
    logit = _dot(g1.astype(BF16), wg2_ref[...]) + bg_ref[...]
    log_a = (jnp.minimum(logit, 0.0) - jnp.log(1.0 + jnp.exp(-jnp.abs(logit)))) * (1.0 / GLA_TAU)
    la_ref[...] = log_a
    col_max = jnp.max(-log_a, axis=0, keepdims=True)
    lamax_ref[...] = jnp.maximum(lamax_ref[...],
                                 jnp.maximum(col_max[:, :LANES], col_max[:, LANES:]))
    off = 0
    for o_ref, scale in zip(group_refs, _IN_SCALES):
        width = o_ref.shape[-1]
        for c0 in range(0, width, 512):
            cw = min(512, width - c0)
            acc = _dot(h, w_ref[:, off + c0:off + c0 + cw])
            if scale != 1.0:
                acc = acc * scale
            o_ref[:, c0:c0 + cw] = acc.astype(o_ref.dtype)
        off += width


def _inproj(x, nw, wcat, wg2, bg, layer):
    n = x.shape[0]
    tm = _INPROJ_TILE
    gate_w = GLA_HEADS * GLA_DK
    row = lambda w: pl.BlockSpec((tm, w), lambda i: (i, 0))
    return pl.pallas_call(
        _inproj_kernel,
        grid=(n // tm,),
        in_specs=[row(D_MODEL), _layer_spec(nw, layer), _layer_spec(wcat, layer),
                  _layer_spec(wg2, layer), _layer_spec(bg, layer)],
        out_specs=[row(w) for w in _IN_WIDTHS]
        + [row(gate_w), pl.BlockSpec((8, LANES), lambda i: (0, 0))],
        out_shape=[jax.ShapeDtypeStruct((n, w), BF16) for w in _IN_WIDTHS]
        + [jax.ShapeDtypeStruct((n, gate_w), F32), jax.ShapeDtypeStruct((8, LANES), F32)],
        compiler_params=_params("arbitrary"),
        name="inproj",
    )(x, nw, wcat, wg2, bg)


_GLA_HALVES = (32, 16, 8, 4, 2, 1)


def _gla_consts():
    c = GLA_CHUNK
    t = np.arange(c)
    tri = (t[None, :] <= t[:, None]).astype(np.float32)
    blocks = [tri]
    masks = []
    for m in _GLA_HALVES:
        ref = (t // (2 * m)) * (2 * m) + m - 1
        blocks.append(tri[ref])
        same = (t[:, None] // (2 * m)) == (t[None, :] // (2 * m))
        upper = (t[:, None] % (2 * m)) >= m
        lower = (t[None, :] % (2 * m)) < m
        masks.append((same & upper & lower).astype(np.float32))
    blocks.append(tri[np.full(c, c - 1)])
    masks.append(np.eye(c, dtype=np.float32))
    sel = np.concatenate(blocks, 0)
    return np.concatenate([sel, sel], 1), np.stack(masks)


def _gla_consts_fast():
    c = GLA_CHUNK
    t = np.arange(c)
    tri = (t[None, :] <= t[:, None]).astype(np.float32)
    sel = np.concatenate([tri, tri[np.full(c, c - 1)]], 0)
    return np.concatenate([sel, sel], 1), tri[None]


def _gla_kernel(qk_ref, v_ref, la_ref, r_ref, nw_ref, sel_ref, mask_ref,
                o_ref, st_ref, *, n_chunks, fast):
    c = GLA_CHUNK
    nlev = 0 if fast else len(_GLA_HALVES)

    @pl.when(pl.program_id(1) == 0)
    def _():
        st_ref[...] = jnp.zeros_like(st_ref)

    lane = lax.broadcasted_iota(jnp.int32, (c, LANES), 1)
    first_half = lane < GLA_DK
    lane_st = lax.broadcasted_iota(jnp.int32, (GLA_DV, LANES), 1) < GLA_DK

    chunks = range(n_chunks)
    heads = range(GLA_HEADS)
    keep = [first_half if h % 2 == 0 else jnp.logical_not(first_half) for h in heads]

    def pair(x, h):
        return x[:, (h // 2) * LANES:(h // 2 + 1) * LANES]

    def masked(x, h):
        xp = pair(x, h)
        return jnp.where(keep[h], xp, jnp.zeros_like(xp))

    qk = qk_ref[0]
    q_all = qk[:, :256].astype(F32) * (GLA_DK ** -0.5)
    k_all = qk[:, 256:].astype(F32)
    log_a = la_ref[0]
    la_hi = log_a.astype(BF16)
    la_lo = (log_a - la_hi.astype(F32)).astype(BF16)
    rows = [slice(ci * c, (ci + 1) * c) for ci in chunks]
    q = [q_all[r] for r in rows]
    k = [k_all[r] for r in rows]
    v = [v_ref[0, r, :] for r in rows]
    rb = [_dot(sel_ref[...], jnp.concatenate([la_hi[r], la_lo[r]], axis=0)) for r in rows]
    bc = [x[0:c] for x in rb]
    b_last = [x[(nlev + 1) * c:(nlev + 2) * c] for x in rb]
    q_dec = [(q[ci] * jnp.exp(bc[ci])).astype(BF16) for ci in chunks]
    k_dec = [(k[ci] * jnp.exp(b_last[ci] - bc[ci])).astype(BF16) for ci in chunks]
    st_decay = [jnp.exp(b_last[ci][0:1, :]) for ci in chunks]

    if fast:
        causal = mask_ref[0] > 0.5
        k_grow = [(k[ci] * jnp.exp(-bc[ci])).astype(BF16) for ci in chunks]
        a = [[jnp.where(causal, _dot_nt(masked(q_dec[ci], h), pair(k_grow[ci], h)), 0.0)
              for h in heads] for ci in chunks]
    else:
        a = [[jnp.zeros((c, c), F32) for _ in heads] for _ in chunks]
        for lv in range(nlev + 1):
            msk = mask_ref[lv]
            for ci in chunks:
                if lv < nlev:
                    w = jnp.exp(-jnp.abs(bc[ci] - rb[ci][(lv + 1) * c:(lv + 2) * c]))
                    qs = (q[ci] * w).astype(BF16)
                    ks = (k[ci] * w).astype(BF16)
                else:
                    qs = q[ci].astype(BF16)
                    ks = k[ci].astype(BF16)
                for h in heads:
                    a[ci][h] = a[ci][h] + _dot_nt(masked(qs, h), pair(ks, h)) * msk

    v_h = [[v[ci][:, h * GLA_DV:(h + 1) * GLA_DV] for h in heads] for ci in chunks]
    o_intra = [[_dot(a[ci][h].astype(BF16), v_h[ci][h]) for h in heads] for ci in chunks]
    upd = [[_dot_tn(v_h[ci][h], pair(k_dec[ci], h)) for h in heads] for ci in chunks]

    st = st_ref[...]
    for ci in chunks:
        st_b = st.astype(BF16)
        for h in heads:
            o = o_intra[ci][h] + _dot_nt(masked(q_dec[ci], h), pair(st_b, h))
            ms = jnp.mean(o * o, axis=-1, keepdims=True)
            on = o * lax.rsqrt(ms + EPS) * nw_ref[...]
            rg = r_ref[0, rows[ci], h * GLA_DV:(h + 1) * GLA_DV].astype(F32)
            o_ref[0, rows[ci], h * GLA_DV:(h + 1) * GLA_DV] = (on * (rg * _sigmoid(rg))).astype(o_ref.dtype)
        st = st_decay[ci] * st + jnp.concatenate(
            [jnp.where(lane_st, upd[ci][2 * pr], upd[ci][2 * pr + 1]) for pr in range(GLA_HEADS // 2)],
            axis=1)
    st_ref[...] = st


def _gla(qk, v, la, r, nw, layer, fast, tg=1024):
    b, t, _ = qk.shape
    sel_np, mask_np = _gla_consts_fast() if fast else _gla_consts()
    sel = jnp.asarray(sel_np, BF16)
    masks = jnp.asarray(mask_np, F32)
    kern = functools.partial(_gla_kernel, n_chunks=tg // GLA_CHUNK, fast=fast)
    seq = lambda w: pl.BlockSpec((1, tg, w), lambda bi, i: (bi, i, 0))
    return pl.pallas_call(
        kern,
        grid=(b, t // tg),
        in_specs=[seq(512), seq(512), seq(GLA_HEADS * GLA_DK), seq(512),
                  _layer_spec(nw, layer), _const_spec(sel.shape), _const_spec(masks.shape)],
        out_specs=seq(512),
        out_shape=jax.ShapeDtypeStruct((b, t, D_MIX), BF16),
        scratch_shapes=[pltpu.VMEM((GLA_DV, GLA_HEADS * GLA_DK), F32)],
        compiler_params=_params("arbitrary", "arbitrary"),
        name="gla_fast" if fast else "gla",
    )(qk, v, la, r, nw, sel, masks)


def _moba_consts():
    blk = MOBA_BLOCK
    tq = _MOBA_QTILES * blk
    c = (2.0 ** (-8.0 * (np.arange(MOBA_HEADS) + 1) / MOBA_HEADS) * LOG2E).astype(np.float32)
    pos = np.arange(blk, dtype=np.float32)
    kx = np.zeros((blk, LANES), np.float32)
    kx[:, 0:3] = pos[:, None]
    pieces = []
    rest = c.copy()
    for _ in range(3):
        piece = rest.astype(jnp.bfloat16).astype(np.float32)
        pieces.append(piece)
        rest = rest - piece
    qx = np.zeros((MOBA_HEADS, LANES, tq), np.float32)
    for j, piece in enumerate(pieces):
        qx[:, j, :] = piece[:, None]
    causal = np.where(pos[:, None] <= pos[None, :], 0.0, NEG).astype(np.float32)
    masks = np.zeros((_MOBA_QTILES + 1, blk, tq), np.float32)
    for t in range(_MOBA_QTILES):
        masks[t + 1, :, t * blk:(t + 1) * blk] = causal
    ucol = c[:, None] * np.arange(tq, dtype=np.float32)[None, :]
    return c, kx, qx, masks, ucol.astype(np.float32)


_MOBA_VROWS = MOBA_DH + 16
_MOBA_UNROLL = 4
_MOBA_QTILES = 2


def _moba_kernel(q_ref, k_ref, v_ref, kx_ref, qx_ref, mask_ref, ucol_ref, o_ref,
                 kmean_ref, vt_ref, acc_ref, m_ref, sel_ref, qa_ref, z_ref, bmax_ref,
                 *, nb, slopes):
    blk = MOBA_BLOCK
    qt = _MOBA_QTILES
    tq = qt * blk
    i0 = pl.program_id(1) * qt

    @pl.when(i0 == 0)
    def _():
        for n in range(nb):
            kb = k_ref[0, n * blk:(n + 1) * blk, :].astype(F32)
            kmean_ref[n:n + 1, :] = jnp.sum(kb, axis=0, keepdims=True) * (1.0 / blk)
            vbt = v_ref[0, n * blk:(n + 1) * blk, :].astype(F32).T
            for h in range(MOBA_HEADS):
                vt_ref[h, 0:MOBA_DH, n * blk:(n + 1) * blk] = (
                    vbt[h * MOBA_DH:(h + 1) * MOBA_DH, :].astype(BF16))
                vt_ref[h, MOBA_DH:_MOBA_VROWS, n * blk:(n + 1) * blk] = jnp.ones(
                    (_MOBA_VROWS - MOBA_DH, blk), BF16)

    lane = lax.broadcasted_iota(jnp.int32, (tq, LANES), 1)
    first_half = lane < MOBA_DH
    rowid = lax.broadcasted_iota(jnp.int32, (nb, tq), 0)
    own_blk = i0 + lax.broadcasted_iota(jnp.int32, (1, tq), 1) // blk

    acc_ref[...] = jnp.zeros_like(acc_ref)
    m_ref[...] = jnp.full_like(m_ref, NEG)

    def scores(h, n, slot, own):
        pr = h // 2
        kp = k_ref[0, pl.ds(pl.multiple_of(n * blk, blk), blk), pr * LANES:(pr + 1) * LANES]
        z = _dot(jnp.concatenate([kp, kx_ref[...]], axis=1), qa_ref[h])
        if not (isinstance(own, int) and own == 0):
            z = z + mask_ref[own]
        z_ref[slot, h] = z
        bmax_ref[slot, h:h + 1, :] = jnp.max(z, axis=0, keepdims=True)

    for h in range(MOBA_HEADS):
        pr = h // 2
        qp = q_ref[0, :, pr * LANES:(pr + 1) * LANES]
        keep = first_half if h % 2 == 0 else jnp.logical_not(first_half)
        qm = jnp.where(keep, qp, jnp.zeros_like(qp))
        qmt = qm.astype(F32).T.astype(BF16)
        qa_ref[h, 0:LANES, :] = qmt
        qa_ref[h, LANES:2 * LANES, :] = qx_ref[h]
        km = kmean_ref[:, pr * LANES:(pr + 1) * LANES]
        km_hi = km.astype(BF16)
        km_lo = (km - km_hi.astype(F32)).astype(BF16)
        g = _dot(km_hi, qmt) + _dot(km_lo, qmt)
        g = jnp.where(rowid < own_blk, g, NEG)
        sel = jnp.where(rowid == own_blk, 1.0, 0.0)
        for r in range(MOBA_TOPK):
            mx = jnp.max(g, axis=0, keepdims=True)
            idx = jnp.min(jnp.where(g == mx, rowid, nb), axis=0, keepdims=True)
            pick = rowid == idx
            sel = jnp.where(jnp.logical_and(pick, own_blk > r), 1.0, sel)
            g = jnp.where(pick, -3.0e38, g)
        sel_ref[h] = sel
        scores(h, 0, 0, (i0 == 0).astype(jnp.int32))

    def accumulate(h, n, slot):
        u = ucol_ref[h:h + 1, :] + (slopes[h] * blk) * (i0 - n).astype(F32)
        picked = sel_ref[h, pl.ds(n, 1), :] > 0.5
        m_old = m_ref[h:h + 1, :]
        m_new = jnp.where(picked, jnp.maximum(m_old, bmax_ref[slot, h:h + 1, :] - u), m_old)
        off = jnp.where(picked, m_new + u, -NEG)
        p = jnp.exp2(z_ref[slot, h] - off)
        alpha = jnp.exp2(m_old - m_new)
        m_ref[h:h + 1, :] = m_new
        vt = vt_ref[h, :, pl.ds(pl.multiple_of(n * blk, blk), blk)]
        acc_ref[h] = alpha * acc_ref[h] + _dot(vt, p.astype(BF16))

    def step(score_args, acc_args):
        for h in range(MOBA_HEADS):
            if score_args is not None:
                scores(h, *score_args)
            if acc_args is not None:
                accumulate(h, *acc_args)

    def regular_steps(k0, count):
        for k in range(count):
            step((k0 + k, (k + 1) % 2, 0), (k0 + k - 1, k % 2))

    def own_steps(first_tile, parity):
        for t in range(first_tile, qt):
            slot = (parity + t - first_tile) % 2
            step((i0 + t, slot, t + 1), (i0 + t - 1, 1 - slot))
        step(None, (i0 + qt - 1, (parity + qt - 1 - first_tile) % 2))

    @pl.when(i0 == 0)
    def _():
        own_steps(1, 1)

    def trip(j, carry):
        regular_steps(_MOBA_UNROLL * j + 1, _MOBA_UNROLL)
        return carry

    n_regular = jnp.maximum(i0 - 1, 0)
    n_trips = n_regular // _MOBA_UNROLL
    lax.fori_loop(0, n_trips, trip, 0)
    k1 = n_trips * _MOBA_UNROLL + 1
    for rem in range(_MOBA_UNROLL):
        if (rem + 1) % math.gcd(_MOBA_UNROLL, qt) != 0:
            continue
        @pl.when(jnp.logical_and(i0 > 0, n_regular % _MOBA_UNROLL == rem))
        def _(rem=rem):
            regular_steps(k1, rem)
            own_steps(0, (rem + 1) % 2)

    out_t = jnp.concatenate(
        [acc_ref[h, 0:MOBA_DH, :] / acc_ref[h, MOBA_DH:MOBA_DH + 1, :] for h in range(MOBA_HEADS)], axis=0)
    o_ref[0] = out_t.T.astype(o_ref.dtype)


def _moba(q, k, v):
    b, t, _ = q.shape
    blk = MOBA_BLOCK
    nb = t // blk
    tq = _MOBA_QTILES * blk
    assert nb % _MOBA_QTILES == 0
    slopes, kx, qx, masks, ucol = _moba_consts()
    kern = functools.partial(_moba_kernel, nb=nb, slopes=tuple(float(s) for s in slopes))
    full = pl.BlockSpec((1, t, D_MIX), lambda bi, i: (bi, 0, 0))
    return pl.pallas_call(
        kern,
        grid=(b, nb // _MOBA_QTILES),
        in_specs=[pl.BlockSpec((1, tq, D_MIX), lambda bi, i: (bi, i, 0)), full, full,
                  _const_spec(kx.shape), _const_spec(qx.shape), _const_spec(masks.shape),
                  _const_spec(ucol.shape)],
        out_specs=pl.BlockSpec((1, tq, D_MIX), lambda bi, i: (bi, i, 0)),
        out_shape=jax.ShapeDtypeStruct((b, t, D_MIX), BF16),
        scratch_shapes=[pltpu.VMEM((nb, D_MIX), F32),
                        pltpu.VMEM((MOBA_HEADS, _MOBA_VROWS, t), BF16),
                        pltpu.VMEM((MOBA_HEADS, _MOBA_VROWS, tq), F32),
                        pltpu.VMEM((MOBA_HEADS, tq), F32),
                        pltpu.VMEM((MOBA_HEADS, nb, tq), F32),
                        pltpu.VMEM((MOBA_HEADS, 2 * LANES, tq), BF16),
                        pltpu.VMEM((2, MOBA_HEADS, blk, tq), F32),
                        pltpu.VMEM((2, MOBA_HEADS, tq), F32)],
        compiler_params=_params("arbitrary", "arbitrary"),
        name="moba",
    )(q, k, v, jnp.asarray(kx, BF16), jnp.asarray(qx, BF16), jnp.asarray(masks), jnp.asarray(ucol))


_POOL_SUB = 128


def _pool_bands():
    r = np.arange(_POOL_SUB)[:, None] + _POOL_SUB
    c = np.arange(2 * _POOL_SUB)[None, :]
    return np.stack([((r - c >= 0) & (r - c < w)).astype(np.float32) for w in POOL_WINDOWS])


def _mix_kernel(x_ref, oa_ref, oc_ref, pu_ref, halo_ref, gates_ref, band_ref, pw_ref, ps_ref,
                wa_ref, wb_ref, wc_ref, wo_ref, nw_ref, o_ref, *, seq_len):
    tm = x_ref.shape[0]
    sub = _POOL_SUB
    t0 = (pl.program_id(0) * tm) % seq_len
    halo = halo_ref[...]
    halo = jnp.where(t0 > 0, halo, jnp.zeros_like(halo))
    ext = jnp.concatenate([halo, pu_ref[...]], axis=0)
    n_sub = tm // sub
    sums = [[_dot(band_ref[g], ext[j * sub:(j + 2) * sub, g * POOL_GC:(g + 1) * POOL_GC])
             for g in range(len(POOL_WINDOWS))] for j in range(n_sub)]
    tpos = t0 + lax.broadcasted_iota(jnp.int32, (tm, 1), 0)
    yg = []
    for g, w in enumerate(POOL_WINDOWS):
        s = jnp.concatenate([sums[j][g] for j in range(n_sub)], axis=0)
        cnt = jnp.minimum(tpos + 1, w).astype(F32)
        p = s / cnt - pu_ref[:, g * POOL_GC:(g + 1) * POOL_GC].astype(F32)
        yg.append(_dot(p.astype(BF16), pw_ref[g]))
    y = (jnp.concatenate(yg, axis=1) * ps_ref[...]).astype(BF16)

    ya = _dot(oa_ref[...], wa_ref[...])
    yb = _dot(y, wb_ref[...])
    yc = _dot(oc_ref[...], wc_ref[...])
    d = D_MODEL
    mixed = 0.5 * (jnp.tanh(gates_ref[:, 0:d].astype(F32)) * ya
                   + jnp.tanh(gates_ref[:, d:2 * d].astype(F32)) * yb
                   + jnp.tanh(gates_ref[:, 2 * d:3 * d].astype(F32)) * yc
                   + (ya + yb + yc))
    m2 = _dot(mixed.astype(BF16), wo_ref[...])
    o_ref[...] = x_ref[...] + _rms(m2, nw_ref[...])


def _mix(x, oa, oc, pu, gates, pool_w, pool_scale, wa, wb, wc, wo, nw, layer, seq_len):
    n = x.shape[0]
    tm = _ROW_TILE
    sub = _POOL_SUB
    bands = jnp.asarray(_pool_bands(), BF16)
    row = lambda w: pl.BlockSpec((tm, w), lambda i: (i, 0))
    halo = pl.BlockSpec((sub, D_MIX), lambda i: (jnp.maximum(i * (tm // sub) - 1, 0), 0))
    params = (pool_w, pool_scale, wa, wb, wc, wo, nw)
    return pl.pallas_call(
        functools.partial(_mix_kernel, seq_len=seq_len),
        grid=(n // tm,),
        in_specs=[row(D_MODEL), row(D_MIX), row(D_MIX), row(D_MIX), halo, row(3 * D_MODEL),
                  _const_spec(bands.shape)] + [_layer_spec(p, layer) for p in params],
        out_specs=row(D_MODEL),
        out_shape=jax.ShapeDtypeStruct((n, D_MODEL), F32),
        compiler_params=_params("arbitrary"),
        name="mix",
    )(x, oa, oc, pu, pu, gates, bands, *params)


def _ffn_kernel(x_ref, npre_ref, wg_ref, wu_ref, wd_ref, npost_ref, o_ref):
    sub = _FFN_SUB
    for r0 in range(0, x_ref.shape[0], sub):
        r = slice(r0, r0 + sub)
        x = x_ref[r, :]
        h = _rms(x, npre_ref[...]).astype(BF16)
        g = _dot(h, wg_ref[...])
        u = _dot(h, wu_ref[...])
        a = (g * _sigmoid(g) * u).astype(BF16)
        f = _dot(a, wd_ref[...])
        o_ref[r, :] = x + _rms(f, npost_ref[...])


def _ffn(x, npre, wg, wu, wd, npost, layer):
    n = x.shape[0]
    tm = _FFN_TILE
    row = pl.BlockSpec((tm, D_MODEL), lambda i: (i, 0))
    params = (npre, wg, wu, wd, npost)
    return pl.pallas_call(
        _ffn_kernel,
        grid=(n // tm,),
        in_specs=[row] + [_layer_spec(p, layer) for p in params],
        out_specs=row,
        out_shape=jax.ShapeDtypeStruct((n, D_MODEL), F32),
        compiler_params=_params("arbitrary"),
        name="ffn",
    )(x, *params)


def _prep_w_in(w):
    sizes = (256, 256, 512, GLA_RANK, 512, 512, 512, 512, 512, 3 * D_MODEL)
    offs = np.concatenate([[0], np.cumsum(sizes)])
    gq, gk, gv, g1, gr, pu, mq, mk, mv, gates = [w[..., offs[j]:offs[j + 1]] for j in range(len(sizes))]
    g1p = jnp.pad(g1, ((0, 0), (0, 0), (0, LANES - GLA_RANK)))
    return jnp.concatenate([gq, gk, gv, gr, pu, mq, mk, mv, gates, g1p], axis=-1).astype(BF16)


def kernel(x, norm_mix_pre, w_in, gla_w_g2, gla_b_g, gla_norm, pool_w, pool_scale, w_branch_a,
           w_branch_b, w_branch_c, w_out, norm_mix_post, norm_ffn_pre, ffn_w_gate, ffn_w_up,
           ffn_w_down, norm_ffn_post):
    b, t, d = x.shape
    n = b * t
    xf = x.reshape(n, d)
    seq = lambda a: a.reshape(b, t, a.shape[-1])
    row = lambda a: a[:, None, :]
    bf = lambda a: a.astype(BF16)
    w_cat = _prep_w_in(w_in)
    wg2 = bf(jnp.pad(gla_w_g2, ((0, 0), (0, LANES - GLA_RANK), (0, 0))))
    mix_params = (bf(pool_w), row(pool_scale), bf(w_branch_a), bf(w_branch_b), bf(w_branch_c),
                  bf(w_out), row(norm_mix_post))
    ffn_params = (row(norm_ffn_pre), bf(ffn_w_gate), bf(ffn_w_up), bf(ffn_w_down), row(norm_ffn_post))
    gla_nw = row(gla_norm)
    for l in range(w_in.shape[0]):
        gqk, gv, gr, pu, mq, mk, mv, gates, la, la_max = _inproj(
            xf, row(norm_mix_pre), w_cat, wg2, row(gla_b_g), l)
        gla_args = (seq(gqk), seq(gv), seq(la), seq(gr), gla_nw)
        bounded = jnp.max(la_max) * (GLA_CHUNK - 1) <= _GLA_FAST_MAX_DECAY
        oa = lax.cond(bounded,
                      functools.partial(_gla, layer=l, fast=True),
                      functools.partial(_gla, layer=l, fast=False), *gla_args)
        oc = _moba(seq(mq), seq(mk), seq(mv))
        xf = _mix(xf, oa.reshape(n, D_MIX), oc.reshape(n, D_MIX), pu, gates, *mix_params, l, t)
        xf = _ffn(xf, *ffn_params, l)
    return xf.reshape(b, t, d)
```

```python
import functools
import math

import numpy as np
import jax
import jax.numpy as jnp
from jax import lax
from jax.experimental import pallas as pl
from jax.experimental.pallas import tpu as pltpu

F32 = jnp.float32
BF16 = jnp.bfloat16

D_MODEL = 1024
D_MIX = 512
GLA_HEADS = 4
GLA_DK = 64
GLA_DV = 128
GLA_RANK = 16
GLA_TAU = 16.0
GLA_CHUNK = 64
POOL_WINDOWS = (2, 4, 8, 16)
POOL_GC = 128
MOBA_HEADS = 8
MOBA_DH = 64
MOBA_BLOCK = 256
MOBA_TOPK = 3
D_FF = 2816
EPS = 1e-6
NEG = -1e30

LANES = 128
VMEM_LIMIT = 56 * 1024 * 1024

_IN_WIDTHS = (512, 512, 512, 512, 512, 512, 512, 3072)
LOG2E = 1.4426950408889634
_IN_SCALES = (1.0, 1.0, 1.0, 1.0, MOBA_DH ** -0.5 * LOG2E, 1.0, 1.0, 0.5)
_ROW_TILE = 512
_INPROJ_TILE = 1024
_FFN_TILE = 1024
_FFN_SUB = 256
_GLA_FAST_MAX_DECAY = 40.0


def _dot(a, b):
    return jnp.dot(a, b, preferred_element_type=F32)


def _dot_nt(a, b):
    return lax.dot_general(a, b, (((1,), (1,)), ((), ())), preferred_element_type=F32)


def _dot_tn(a, b):
    return lax.dot_general(a, b, (((0,), (0,)), ((), ())), preferred_element_type=F32)


def _rms(x, w):
    ms = jnp.mean(x * x, axis=-1, keepdims=True)
    return x * lax.rsqrt(ms + EPS) * w


def _sigmoid(x):
    return 1.0 / (1.0 + jnp.exp(-x))


def _params(*sem):
    return pltpu.CompilerParams(dimension_semantics=sem, vmem_limit_bytes=VMEM_LIMIT)


def _const_spec(shape):
    nd = len(shape)
    return pl.BlockSpec(shape, lambda *_: (0,) * nd, pipeline_mode=pl.Buffered(1))


def _layer_spec(stacked, layer):
    nd = stacked.ndim - 1
    return pl.BlockSpec((None,) + stacked.shape[1:], lambda *_: (layer,) + (0,) * nd,
                        pipeline_mode=pl.Buffered(1))


def _inproj_kernel(x_ref, nw_ref, w_ref, wg2_ref, bg_ref, *o_refs):
    *group_refs, la_ref, lamax_ref = o_refs

    @pl.when(pl.program_id(0) == 0)
    def _():
        lamax_ref[...] = jnp.zeros_like(lamax_ref)

    h = _rms(x_ref[...], nw_ref[...]).astype(BF16)
    g1 = _dot(h, w_ref[:, sum(_IN_WIDTHS):])
    logit = _dot(g1.astype(BF16), wg2_ref[...]) + bg_ref[...]
    log_a = (jnp.minimum(logit, 0.0) - jnp.log(1.0 + jnp.exp(-jnp.abs(logit)))) * (1.0 / GLA_TAU)
    la_ref[...] = log_a
    col_max = jnp.max(-log_a, axis=0, keepdims=True)
    lamax_ref[...] = jnp.maximum(lamax_ref[...],
                                 jnp.maximum(col_max[:, :LANES], col_max[:, LANES:]))
    off = 0
    for o_ref, scale in zip(group_refs, _IN_SCALES):
        width = o_ref.shape[-1]
        for c0 in range(0, width, 512):
            cw = min(512, width - c0)
            acc = _dot(h, w_ref[:, off + c0:off + c0 + cw])
            if scale != 1.0:
                acc = acc * scale
            o_ref[:, c0:c0 + cw] = acc.astype(o_ref.dtype)
        off += width


def _inproj(x, nw, wcat, wg2, bg, layer):
    n = x.shape[0]
    tm = _INPROJ_TILE
    gate_w = GLA_HEADS * GLA_DK
    row = lambda w: pl.BlockSpec((tm, w), lambda i: (i, 0))
    return pl.pallas_call(
        _inproj_kernel,
        grid=(n // tm,),
        in_specs=[row(D_MODEL), _layer_spec(nw, layer), _layer_spec(wcat, layer),
                  _layer_spec(wg2, layer), _layer_spec(bg, layer)],
        out_specs=[row(w) for w in _IN_WIDTHS]
        + [row(gate_w), pl.BlockSpec((8, LANES), lambda i: (0, 0))],
        out_shape=[jax.ShapeDtypeStruct((n, w), BF16) for w in _IN_WIDTHS]
        + [jax.ShapeDtypeStruct((n, gate_w), F32), jax.ShapeDtypeStruct((8, LANES), F32)],
        compiler_params=_params("arbitrary"),
        name="inproj",
    )(x, nw, wcat, wg2, bg)


_GLA_HALVES = (32, 16, 8, 4, 2, 1)


def _gla_consts():
    c = GLA_CHUNK
    t = np.arange(c)
    tri = (t[None, :] <= t[:, None]).astype(np.float32)
    blocks = [tri]
    masks = []
    for m in _GLA_HALVES:
        ref = (t // (2 * m)) * (2 * m) + m - 1
        blocks.append(tri[ref])
        same = (t[:, None] // (2 * m)) == (t[None, :] // (2 * m))
        upper = (t[:, None] % (2 * m)) >= m
        lower = (t[None, :] % (2 * m)) < m
        masks.append((same & upper & lower).astype(np.float32))
    blocks.append(tri[np.full(c, c - 1)])
    masks.append(np.eye(c, dtype=np.float32))
    sel = np.concatenate(blocks, 0)
    return np.concatenate([sel, sel], 1), np.stack(masks)


def _gla_consts_fast():
    c = GLA_CHUNK
    t = np.arange(c)
    tri = (t[None, :] <= t[:, None]).astype(np.float32)
    sel = np.concatenate([tri, tri[np.full(c, c - 1)]], 0)
    return np.concatenate([sel, sel], 1), tri[None]


def _gla_kernel(qk_ref, v_ref, la_ref, r_ref, nw_ref, sel_ref, mask_ref,
                o_ref, st_ref, *, n_chunks, fast):
    c = GLA_CHUNK
    nlev = 0 if fast else len(_GLA_HALVES)

    @pl.when(pl.program_id(1) == 0)
    def _():
        st_ref[...] = jnp.zeros_like(st_ref)

    lane = lax.broadcasted_iota(jnp.int32, (c, LANES), 1)
    first_half = lane < GLA_DK
    lane_st = lax.broadcasted_iota(jnp.int32, (GLA_DV, LANES), 1) < GLA_DK

    chunks = range(n_chunks)
    heads = range(GLA_HEADS)
    keep = [first_half if h % 2 == 0 else jnp.logical_not(first_half) for h in heads]

    def pair(x, h):
        return x[:, (h // 2) * LANES:(h // 2 + 1) * LANES]

    def masked(x, h):
        xp = pair(x, h)
        return jnp.where(keep[h], xp, jnp.zeros_like(xp))

    qk = qk_ref[0]
    q_all = qk[:, :256].astype(F32) * (GLA_DK ** -0.5)
    k_all = qk[:, 256:].astype(F32)
    log_a = la_ref[0]
    la_hi = log_a.astype(BF16)
    la_lo = (log_a - la_hi.astype(F32)).astype(BF16)
    rows = [slice(ci * c, (ci + 1) * c) for ci in chunks]
    q = [q_all[r] for r in rows]
    k = [k_all[r] for r in rows]
    v = [v_ref[0, r, :] for r in rows]
    rb = [_dot(sel_ref[...], jnp.concatenate([la_hi[r], la_lo[r]], axis=0)) for r in rows]
    bc = [x[0:c] for x in rb]
    b_last = [x[(nlev + 1) * c:(nlev + 2) * c] for x in rb]
    q_dec = [(q[ci] * jnp.exp(bc[ci])).astype(BF16) for ci in chunks]
    k_dec = [(k[ci] * jnp.exp(b_last[ci] - bc[ci])).astype(BF16) for ci in chunks]
    st_decay = [jnp.exp(b_last[ci][0:1, :]) for ci in chunks]

    if fast:
        causal = mask_ref[0] > 0.5
        k_grow = [(k[ci] * jnp.exp(-bc[ci])).astype(BF16) for ci in chunks]
        a = [[jnp.where(causal, _dot_nt(masked(q_dec[ci], h), pair(k_grow[ci], h)), 0.0)
              for h in heads] for ci in chunks]
    else:
        a = [[jnp.zeros((c, c), F32) for _ in heads] for _ in chunks]
        for lv in range(nlev + 1):
            msk = mask_ref[lv]
            for ci in chunks:
                if lv < nlev:
                    w = jnp.exp(-jnp.abs(bc[ci] - rb[ci][(lv + 1) * c:(lv + 2) * c]))
                    qs = (q[ci] * w).astype(BF16)
                    ks = (k[ci] * w).astype(BF16)
                else:
                    qs = q[ci].astype(BF16)
                    ks = k[ci].astype(BF16)
                for h in heads:
                    a[ci][h] = a[ci][h] + _dot_nt(masked(qs, h), pair(ks, h)) * msk

    v_h = [[v[ci][:, h * GLA_DV:(h + 1) * GLA_DV] for h in heads] for ci in chunks]
    o_intra = [[_dot(a[ci][h].astype(BF16), v_h[ci][h]) for h in heads] for ci in chunks]
    upd = [[_dot_tn(v_h[ci][h], pair(k_dec[ci], h)) for h in heads] for ci in chunks]

    st = st_ref[...]
    for ci in chunks:
        st_b = st.astype(BF16)
        for h in heads:
            o = o_intra[ci][h] + _dot_nt(masked(q_dec[ci], h), pair(st_b, h))
            ms = jnp.mean(o * o, axis=-1, keepdims=True)
            on = o * lax.rsqrt(ms + EPS) * nw_ref[...]
            rg = r_ref[0, rows[ci], h * GLA_DV:(h + 1) * GLA_DV].astype(F32)
            o_ref[0, rows[ci], h * GLA_DV:(h + 1) * GLA_DV] = (on * (rg * _sigmoid(rg))).astype(o_ref.dtype)
        st = st_decay[ci] * st + jnp.concatenate(
            [jnp.where(lane_st, upd[ci][2 * pr], upd[ci][2 * pr + 1]) for pr in range(GLA_HEADS // 2)],
            axis=1)
    st_ref[...] = st


def _gla(qk, v, la, r, nw, layer, fast, tg=1024):
    b, t, _ = qk.shape
    sel_np, mask_np = _gla_consts_fast() if fast else _gla_consts()
    sel = jnp.asarray(sel_np, BF16)
    masks = jnp.asarray(mask_np, F32)
    kern = functools.partial(_gla_kernel, n_chunks=tg // GLA_CHUNK, fast=fast)
    seq = lambda w: pl.BlockSpec((1, tg, w), lambda bi, i: (bi, i, 0))
    return pl.pallas_call(
        kern,
        grid=(b, t // tg),
        in_specs=[seq(512), seq(512), seq(GLA_HEADS * GLA_DK), seq(512),
                  _layer_spec(nw, layer), _const_spec(sel.shape), _const_spec(masks.shape)],
        out_specs=seq(512),
        out_shape=jax.ShapeDtypeStruct((b, t, D_MIX), BF16),
        scratch_shapes=[pltpu.VMEM((GLA_DV, GLA_HEADS * GLA_DK), F32)],
        compiler_params=_params("arbitrary", "arbitrary"),
        name="gla_fast" if fast else "gla",
    )(qk, v, la, r, nw, sel, masks)


def _moba_consts():
    blk = MOBA_BLOCK
    tq = _MOBA_QTILES * blk
    c = (2.0 ** (-8.0 * (np.arange(MOBA_HEADS) + 1) / MOBA_HEADS) * LOG2E).astype(np.float32)
    pos = np.arange(blk, dtype=np.float32)
    kx = np.zeros((blk, LANES), np.float32)
    kx[:, 0:3] = pos[:, None]
    pieces = []
    rest = c.copy()
    for _ in range(3):
        piece = rest.astype(jnp.bfloat16).astype(np.float32)
        pieces.append(piece)
        rest = rest - piece
    qx = np.zeros((MOBA_HEADS, LANES, tq), np.float32)
    for j, piece in enumerate(pieces):
        qx[:, j, :] = piece[:, None]
    causal = np.where(pos[:, None] <= pos[None, :], 0.0, NEG).astype(np.float32)
    masks = np.zeros((_MOBA_QTILES + 1, blk, tq), np.float32)
    for t in range(_MOBA_QTILES):
        masks[t + 1, :, t * blk:(t + 1) * blk] = causal
    ucol = c[:, None] * np.arange(tq, dtype=np.float32)[None, :]
    return c, kx, qx, masks, ucol.astype(np.float32)


_MOBA_VROWS = MOBA_DH + 16
_MOBA_UNROLL = 8
_MOBA_QTILES = 1


def _moba_kernel(q_ref, k_ref, v_ref, kx_ref, qx_ref, mask_ref, ucol_ref, o_ref,
                 kmean_ref, vt_ref, acc_ref, m_ref, sel_ref, qa_ref, z_ref, bmax_ref,
                 *, nb, slopes):
    blk = MOBA_BLOCK
    qt = _MOBA_QTILES
    tq = qt * blk
    i0 = pl.program_id(1) * qt

    @pl.when(i0 == 0)
    def _():
        for n in range(nb):
            kb = k_ref[0, n * blk:(n + 1) * blk, :].astype(F32)
            kmean_ref[n:n + 1, :] = jnp.sum(kb, axis=0, keepdims=True) * (1.0 / blk)
            vbt = v_ref[0, n * blk:(n + 1) * blk, :].astype(F32).T
            for h in range(MOBA_HEADS):
                vt_ref[h, 0:MOBA_DH, n * blk:(n + 1) * blk] = (
                    vbt[h * MOBA_DH:(h + 1) * MOBA_DH, :].astype(BF16))
                vt_ref[h, MOBA_DH:_MOBA_VROWS, n * blk:(n + 1) * blk] = jnp.ones(
                    (_MOBA_VROWS - MOBA_DH, blk), BF16)

    lane = lax.broadcasted_iota(jnp.int32, (tq, LANES), 1)
    first_half = lane < MOBA_DH
    rowid = lax.broadcasted_iota(jnp.int32, (nb, tq), 0)
    own_blk = i0 + lax.broadcasted_iota(jnp.int32, (1, tq), 1) // blk

    acc_ref[...] = jnp.zeros_like(acc_ref)
    m_ref[...] = jnp.full_like(m_ref, NEG)

    def scores(h, n, slot, own):
        pr = h // 2
        kp = k_ref[0, pl.ds(pl.multiple_of(n * blk, blk), blk), pr * LANES:(pr + 1) * LANES]
        z = _dot(jnp.concatenate([kp, kx_ref[...]], axis=1), qa_ref[h])
        if not (isinstance(own, int) and own == 0):
            z = z + mask_ref[own]
        z_ref[slot, h] = z
        bmax_ref[slot, h:h + 1, :] = jnp.max(z, axis=0, keepdims=True)

    for h in range(MOBA_HEADS):
        pr = h // 2
        qp = q_ref[0, :, pr * LANES:(pr + 1) * LANES]
        keep = first_half if h % 2 == 0 else jnp.logical_not(first_half)
        qm = jnp.where(keep, qp, jnp.zeros_like(qp))
        qmt = qm.astype(F32).T.astype(BF16)
        qa_ref[h, 0:LANES, :] = qmt
        qa_ref[h, LANES:2 * LANES, :] = qx_ref[h]
        km = kmean_ref[:, pr * LANES:(pr + 1) * LANES]
        km_hi = km.astype(BF16)
        km_lo = (km - km_hi.astype(F32)).astype(BF16)
        g = _dot(km_hi, qmt) + _dot(km_lo, qmt)
        g = jnp.where(rowid < own_blk, g, NEG)
        sel = jnp.where(rowid == own_blk, 1.0, 0.0)
        for r in range(MOBA_TOPK):
            mx = jnp.max(g, axis=0, keepdims=True)
            idx = jnp.min(jnp.where(g == mx, rowid, nb), axis=0, keepdims=True)
            pick = rowid == idx
            sel = jnp.where(jnp.logical_and(pick, own_blk > r), 1.0, sel)
            g = jnp.where(pick, -3.0e38, g)
        sel_ref[h] = sel
        scores(h, 0, 0, (i0 == 0).astype(jnp.int32))

    def accumulate(h, n, slot):
        u = ucol_ref[h:h + 1, :] + (slopes[h] * blk) * (i0 - n).astype(F32)
        picked = sel_ref[h, pl.ds(n, 1), :] > 0.5
        m_old = m_ref[h:h + 1, :]
        m_new = jnp.where(picked, jnp.maximum(m_old, bmax_ref[slot, h:h + 1, :] - u), m_old)
        off = jnp.where(picked, m_new + u, -NEG)
        p = jnp.exp2(z_ref[slot, h] - off)
        alpha = jnp.exp2(m_old - m_new)
        m_ref[h:h + 1, :] = m_new
        vt = vt_ref[h, :, pl.ds(pl.multiple_of(n * blk, blk), blk)]
        acc_ref[h] = alpha * acc_ref[h] + _dot(vt, p.astype(BF16))

    def step(score_args, acc_args):
        for h in range(MOBA_HEADS):
            if score_args is not None:
                scores(h, *score_args)
            if acc_args is not None:
                accumulate(h, *acc_args)

    def regular_steps(k0, count):
        for k in range(count):
            step((k0 + k, (k + 1) % 2, 0), (k0 + k - 1, k % 2))

    def own_steps(first_tile, parity):
        for t in range(first_tile, qt):
            slot = (parity + t - first_tile) % 2
            step((i0 + t, slot, t + 1), (i0 + t - 1, 1 - slot))
        step(None, (i0 + qt - 1, (parity + qt - 1 - first_tile) % 2))

    @pl.when(i0 == 0)
    def _():
        own_steps(1, 1)

    def trip(j, carry):
        regular_steps(_MOBA_UNROLL * j + 1, _MOBA_UNROLL)
        return carry

    n_regular = jnp.maximum(i0 - 1, 0)
    n_trips = n_regular // _MOBA_UNROLL
    lax.fori_loop(0, n_trips, trip, 0)
    k1 = n_trips * _MOBA_UNROLL + 1
    for rem in range(_MOBA_UNROLL):
        if (rem + 1) % math.gcd(_MOBA_UNROLL, qt) != 0:
            continue
        @pl.when(jnp.logical_and(i0 > 0, n_regular % _MOBA_UNROLL == rem))
        def _(rem=rem):
            regular_steps(k1, rem)
            own_steps(0, (rem + 1) % 2)

    out_t = jnp.concatenate(
        [acc_ref[h, 0:MOBA_DH, :] / acc_ref[h, MOBA_DH:MOBA_DH + 1, :] for h in range(MOBA_HEADS)], axis=0)
    o_ref[0] = out_t.T.astype(o_ref.dtype)


def _moba(q, k, v):
    b, t, _ = q.shape
    blk = MOBA_BLOCK
    nb = t // blk
    tq = _MOBA_QTILES * blk
    assert nb % _MOBA_QTILES == 0
    slopes, kx, qx, masks, ucol = _moba_consts()
    kern = functools.partial(_moba_kernel, nb=nb, slopes=tuple(float(s) for s in slopes))
    full = pl.BlockSpec((1, t, D_MIX), lambda bi, i: (bi, 0, 0))
    return pl.pallas_call(
        kern,
        grid=(b, nb // _MOBA_QTILES),
        in_specs=[pl.BlockSpec((1, tq, D_MIX), lambda bi, i: (bi, i, 0)), full, full,
                  _const_spec(kx.shape), _const_spec(qx.shape), _const_spec(masks.shape),
                  _const_spec(ucol.shape)],
        out_specs=pl.BlockSpec((1, tq, D_MIX), lambda bi, i: (bi, i, 0)),
        out_shape=jax.ShapeDtypeStruct((b, t, D_MIX), BF16),
        scratch_shapes=[pltpu.VMEM((nb, D_MIX), F32),
                        pltpu.VMEM((MOBA_HEADS, _MOBA_VROWS, t), BF16),
                        pltpu.VMEM((MOBA_HEADS, _MOBA_VROWS, tq), F32),
                        pltpu.VMEM((MOBA_HEADS, tq), F32),
                        pltpu.VMEM((MOBA_HEADS, nb, tq), F32),
                        pltpu.VMEM((MOBA_HEADS, 2 * LANES, tq), BF16),
                        pltpu.VMEM((2, MOBA_HEADS, blk, tq), F32),
                        pltpu.VMEM((2, MOBA_HEADS, tq), F32)],
        compiler_params=_params("arbitrary", "arbitrary"),
        name="moba",
    )(q, k, v, jnp.asarray(kx, BF16), jnp.asarray(qx, BF16), jnp.asarray(masks), jnp.asarray(ucol))


_POOL_SUB = 128


def _pool_bands():
    r = np.arange(_POOL_SUB)[:, None] + _POOL_SUB
    c = np.arange(2 * _POOL_SUB)[None, :]
    return np.stack([((r - c >= 0) & (r - c < w)).astype(np.float32) for w in POOL_WINDOWS])


def _mix_kernel(x_ref, oa_ref, oc_ref, pu_ref, halo_ref, gates_ref, band_ref, pw_ref, ps_ref,
                wa_ref, wb_ref, wc_ref, wo_ref, nw_ref, o_ref, *, seq_len):
    tm = x_ref.shape[0]
    sub = _POOL_SUB
    t0 = (pl.program_id(0) * tm) % seq_len
    halo = halo_ref[...]
    halo = jnp.where(t0 > 0, halo, jnp.zeros_like(halo))
    ext = jnp.concatenate([halo, pu_ref[...]], axis=0)
    n_sub = tm // sub
    sums = [[_dot(band_ref[g], ext[j * sub:(j + 2) * sub, g * POOL_GC:(g + 1) * POOL_GC])
             for g in range(len(POOL_WINDOWS))] for j in range(n_sub)]
    tpos = t0 + lax.broadcasted_iota(jnp.int32, (tm, 1), 0)
    yg = []
    for g, w in enumerate(POOL_WINDOWS):
        s = jnp.concatenate([sums[j][g] for j in range(n_sub)], axis=0)
        cnt = jnp.minimum(tpos + 1, w).astype(F32)
        p = s / cnt - pu_ref[:, g * POOL_GC:(g + 1) * POOL_GC].astype(F32)
        yg.append(_dot(p.astype(BF16), pw_ref[g]))
    y = (jnp.concatenate(yg, axis=1) * ps_ref[...]).astype(BF16)

    ya = _dot(oa_ref[...], wa_ref[...])
    yb = _dot(y, wb_ref[...])
    yc = _dot(oc_ref[...], wc_ref[...])
    d = D_MODEL
    mixed = 0.5 * (jnp.tanh(gates_ref[:, 0:d].astype(F32)) * ya
                   + jnp.tanh(gates_ref[:, d:2 * d].astype(F32)) * yb
                   + jnp.tanh(gates_ref[:, 2 * d:3 * d].astype(F32)) * yc
                   + (ya + yb + yc))
    m2 = _dot(mixed.astype(BF16), wo_ref[...])
    o_ref[...] = x_ref[...] + _rms(m2, nw_ref[...])


def _mix(x, oa, oc, pu, gates, pool_w, pool_scale, wa, wb, wc, wo, nw, layer, seq_len):
    n = x.shape[0]
    tm = _ROW_TILE
    sub = _POOL_SUB
    bands = jnp.asarray(_pool_bands(), BF16)
    row = lambda w: pl.BlockSpec((tm, w), lambda i: (i, 0))
    halo = pl.BlockSpec((sub, D_MIX), lambda i: (jnp.maximum(i * (tm // sub) - 1, 0), 0))
    params = (pool_w, pool_scale, wa, wb, wc, wo, nw)
    return pl.pallas_call(
        functools.partial(_mix_kernel, seq_len=seq_len),
        grid=(n // tm,),
        in_specs=[row(D_MODEL), row(D_MIX), row(D_MIX), row(D_MIX), halo, row(3 * D_MODEL),
                  _const_spec(bands.shape)] + [_layer_spec(p, layer) for p in params],
        out_specs=row(D_MODEL),
        out_shape=jax.ShapeDtypeStruct((n, D_MODEL), F32),
        compiler_params=_params("arbitrary"),
        name="mix",
    )(x, oa, oc, pu, pu, gates, bands, *params)


def _ffn_kernel(x_ref, npre_ref, wg_ref, wu_ref, wd_ref, npost_ref, o_ref):
    sub = _FFN_SUB
    for r0 in range(0, x_ref.shape[0], sub):
        r = slice(r0, r0 + sub)
        x = x_ref[r, :]
        h = _rms(x, npre_ref[...]).astype(BF16)
        g = _dot(h, wg_ref[...])
        u = _dot(h, wu_ref[...])
        a = (g * _sigmoid(g) * u).astype(BF16)
        f = _dot(a, wd_ref[...])
        o_ref[r, :] = x + _rms(f, npost_ref[...])


def _ffn(x, npre, wg, wu, wd, npost, layer):
    n = x.shape[0]
    tm = _FFN_TILE
    row = pl.BlockSpec((tm, D_MODEL), lambda i: (i, 0))
    params = (npre, wg, wu, wd, npost)
    return pl.pallas_call(
        _ffn_kernel,
        grid=(n // tm,),
        in_specs=[row] + [_layer_spec(p, layer) for p in params],
        out_specs=row,
        out_shape=jax.ShapeDtypeStruct((n, D_MODEL), F32),
        compiler_params=_params("arbitrary"),
        name="ffn",
    )(x, *params)


def _prep_w_in(w):
    sizes = (256, 256, 512, GLA_RANK, 512, 512, 512, 512, 512, 3 * D_MODEL)
    offs = np.concatenate([[0], np.cumsum(sizes)])
    gq, gk, gv, g1, gr, pu, mq, mk, mv, gates = [w[..., offs[j]:offs[j + 1]] for j in range(len(sizes))]
    g1p = jnp.pad(g1, ((0, 0), (0, 0), (0, LANES - GLA_RANK)))
    return jnp.concatenate([gq, gk, gv, gr, pu, mq, mk, mv, gates, g1p], axis=-1).astype(BF16)


def kernel(x, norm_mix_pre, w_in, gla_w_g2, gla_b_g, gla_norm, pool_w, pool_scale, w_branch_a,
           w_branch_b, w_branch_c, w_out, norm_mix_post, norm_ffn_pre, ffn_w_gate, ffn_w_up,
           ffn_w_down, norm_ffn_post):
    b, t, d = x.shape
    n = b * t
    xf = x.reshape(n, d)
    seq = lambda a: a.reshape(b, t, a.shape[-1])
    row = lambda a: a[:, None, :]
    bf = lambda a: a.astype(BF16)
    w_cat = _prep_w_in(w_in)
    wg2 = bf(jnp.pad(gla_w_g2, ((0, 0), (0, LANES - GLA_RANK), (0, 0))))
    mix_params = (bf(pool_w), row(pool_scale), bf(w_branch_a), bf(w_branch_b), bf(w_branch_c),
                  bf(w_out), row(norm_mix_post))
    ffn_params = (row(norm_ffn_pre), bf(ffn_w_gate), bf(ffn_w_up), bf(ffn_w_down), row(norm_ffn_post))
    gla_nw = row(gla_norm)
    for l in range(w_in.shape[0]):
        gqk, gv, gr, pu, mq, mk, mv, gates, la, la_max = _inproj(
            xf, row(norm_mix_pre), w_cat, wg2, row(gla_b_g), l)
        gla_args = (seq(gqk), seq(gv), seq(la), seq(gr), gla_nw)
        bounded = jnp.max(la_max) * (GLA_CHUNK - 1) <= _GLA_FAST_MAX_DECAY
        oa = lax.cond(bounded,
                      functools.partial(_gla, layer=l, fast=True),
                      functools.partial(_gla, layer=l, fast=False), *gla_args)
        oc = _moba(seq(mq), seq(mk), seq(mv))
        xf = _mix(xf, oa.reshape(n, D_MIX), oc.reshape(n, D_MIX), pu, gates, *mix_params, l, t)
        xf = _ffn(xf, *ffn_params, l)
    return xf.reshape(b, t, d)
```

```python
import functools
import math

import numpy as np
import jax
import jax.numpy as jnp
from jax import lax
from jax.experimental import pallas as pl
from jax.experimental.pallas import tpu as pltpu

F32 = jnp.float32
BF16 = jnp.bfloat16

D_MODEL = 1024
D_MIX = 512
GLA_HEADS = 4
GLA_DK = 64
GLA_DV = 128
GLA_RANK = 16
GLA_TAU = 16.0
GLA_CHUNK = 64
POOL_WINDOWS = (2, 4, 8, 16)
POOL_GC = 128
MOBA_HEADS = 8
MOBA_DH = 64
MOBA_BLOCK = 256
MOBA_TOPK = 3
D_FF = 2816
EPS = 1e-6
NEG = -1e30

LANES = 128
VMEM_LIMIT = 56 * 1024 * 1024

_IN_WIDTHS = (512, 512, 512, 512, 512, 512, 512, 3072)
LOG2E = 1.4426950408889634
_IN_SCALES = (1.0, 1.0, 1.0, 1.0, MOBA_DH ** -0.5 * LOG2E, 1.0, 1.0, 0.5)
_ROW_TILE = 512
_INPROJ_TILE = 1024
_FFN_SUB = 256
_GLA_FAST_MAX_DECAY = 40.0


def _dot(a, b):
    return jnp.dot(a, b, preferred_element_type=F32)


def _dot_nt(a, b):
    return lax.dot_general(a, b, (((1,), (1,)), ((), ())), preferred_element_type=F32)


def _dot_tn(a, b):
    return lax.dot_general(a, b, (((0,), (0,)), ((), ())), preferred_element_type=F32)


def _rms(x, w):
    ms = jnp.mean(x * x, axis=-1, keepdims=True)
    return x * lax.rsqrt(ms + EPS) * w


def _sigmoid(x):
    return 1.0 / (1.0 + jnp.exp(-x))


def _params(*sem):
    return pltpu.CompilerParams(dimension_semantics=sem, vmem_limit_bytes=VMEM_LIMIT)


def _const_spec(shape):
    nd = len(shape)
    return pl.BlockSpec(shape, lambda *_: (0,) * nd, pipeline_mode=pl.Buffered(1))


def _layer_spec(stacked, layer):
    nd = stacked.ndim - 1
    return pl.BlockSpec((None,) + stacked.shape[1:], lambda *_: (layer,) + (0,) * nd,
                        pipeline_mode=pl.Buffered(1))


def _inproj_kernel(x_ref, nw_ref, w_ref, wg2_ref, bg_ref, *o_refs):
    *group_refs, la_ref, lamax_ref = o_refs

    @pl.when(pl.program_id(0) == 0)
    def _():
        lamax_ref[...] = jnp.zeros_like(lamax_ref)

    h = _rms(x_ref[...], nw_ref[...]).astype(BF16)
    g1 = _dot(h, w_ref[:, sum(_IN_WIDTHS):])
    logit = _dot(g1.astype(BF16), wg2_ref[...]) + bg_ref[...]
    log_a = (jnp.minimum(logit, 0.0) - jnp.log(1.0 + jnp.exp(-jnp.abs(logit)))) * (1.0 / GLA_TAU)
    la_ref[...] = log_a
    col_max = jnp.max(-log_a, axis=0, keepdims=True)
    lamax_ref[...] = jnp.maximum(lamax_ref[...],
                                 jnp.maximum(col_max[:, :LANES], col_max[:, LANES:]))
    off = 0
    for o_ref, scale in zip(group_refs, _IN_SCALES):
        width = o_ref.shape[-1]
        for c0 in range(0, width, 512):
            cw = min(512, width - c0)
            acc = _dot(h, w_ref[:, off + c0:off + c0 + cw])
            if scale != 1.0:
                acc = acc * scale
            o_ref[:, c0:c0 + cw] = acc.astype(o_ref.dtype)
        off += width


def _inproj(x, nw, wcat, wg2, bg, layer):
    n = x.shape[0]
    tm = _INPROJ_TILE
    gate_w = GLA_HEADS * GLA_DK
    row = lambda w: pl.BlockSpec((tm, w), lambda i: (i, 0))
    return pl.pallas_call(
        _inproj_kernel,
        grid=(n // tm,),
        in_specs=[row(D_MODEL), _layer_spec(nw, layer), _layer_spec(wcat, layer),
                  _layer_spec(wg2, layer), _layer_spec(bg, layer)],
        out_specs=[row(w) for w in _IN_WIDTHS]
        + [row(gate_w), pl.BlockSpec((8, LANES), lambda i: (0, 0))],
        out_shape=[jax.ShapeDtypeStruct((n, w), BF16) for w in _IN_WIDTHS]
        + [jax.ShapeDtypeStruct((n, gate_w), F32), jax.ShapeDtypeStruct((8, LANES), F32)],
        compiler_params=_params("arbitrary"),
        name="inproj",
    )(x, nw, wcat, wg2, bg)


_GLA_HALVES = (32, 16, 8, 4, 2, 1)


def _gla_consts():
    c = GLA_CHUNK
    t = np.arange(c)
    tri = (t[None, :] <= t[:, None]).astype(np.float32)
    blocks = [tri]
    masks = []
    for m in _GLA_HALVES:
        ref = (t // (2 * m)) * (2 * m) + m - 1
        blocks.append(tri[ref])
        same = (t[:, None] // (2 * m)) == (t[None, :] // (2 * m))
        upper = (t[:, None] % (2 * m)) >= m
        lower = (t[None, :] % (2 * m)) < m
        masks.append((same & upper & lower).astype(np.float32))
    blocks.append(tri[np.full(c, c - 1)])
    masks.append(np.eye(c, dtype=np.float32))
    sel = np.concatenate(blocks, 0)
    return np.concatenate([sel, sel], 1), np.stack(masks)


def _gla_consts_fast():
    c = GLA_CHUNK
    t = np.arange(c)
    tri = (t[None, :] <= t[:, None]).astype(np.float32)
    sel = np.concatenate([tri, tri[np.full(c, c - 1)]], 0)
    return np.concatenate([sel, sel], 1), tri[None]


def _gla_kernel(qk_ref, v_ref, la_ref, r_ref, nw_ref, sel_ref, mask_ref,
                o_ref, st_ref, *, n_chunks, fast):
    c = GLA_CHUNK
    nlev = 0 if fast else len(_GLA_HALVES)

    @pl.when(pl.program_id(1) == 0)
    def _():
        st_ref[...] = jnp.zeros_like(st_ref)

    lane = lax.broadcasted_iota(jnp.int32, (c, LANES), 1)
    first_half = lane < GLA_DK
    lane_st = lax.broadcasted_iota(jnp.int32, (GLA_DV, LANES), 1) < GLA_DK

    chunks = range(n_chunks)
    heads = range(GLA_HEADS)
    keep = [first_half if h % 2 == 0 else jnp.logical_not(first_half) for h in heads]

    def pair(x, h):
        return x[:, (h // 2) * LANES:(h // 2 + 1) * LANES]

    def masked(x, h):
        xp = pair(x, h)
        return jnp.where(keep[h], xp, jnp.zeros_like(xp))

    qk = qk_ref[0]
    q_all = qk[:, :256].astype(F32) * (GLA_DK ** -0.5)
    k_all = qk[:, 256:].astype(F32)
    log_a = la_ref[0]
    la_hi = log_a.astype(BF16)
    la_lo = (log_a - la_hi.astype(F32)).astype(BF16)
    rows = [slice(ci * c, (ci + 1) * c) for ci in chunks]
    q = [q_all[r] for r in rows]
    k = [k_all[r] for r in rows]
    v = [v_ref[0, r, :] for r in rows]
    rb = [_dot(sel_ref[...], jnp.concatenate([la_hi[r], la_lo[r]], axis=0)) for r in rows]
    bc = [x[0:c] for x in rb]
    b_last = [x[(nlev + 1) * c:(nlev + 2) * c] for x in rb]
    q_dec = [(q[ci] * jnp.exp(bc[ci])).astype(BF16) for ci in chunks]
    k_dec = [(k[ci] * jnp.exp(b_last[ci] - bc[ci])).astype(BF16) for ci in chunks]
    st_decay = [jnp.exp(b_last[ci][0:1, :]) for ci in chunks]

    if fast:
        causal = mask_ref[0] > 0.5
        k_grow = [(k[ci] * jnp.exp(-bc[ci])).astype(BF16) for ci in chunks]
        a = [[jnp.where(causal, _dot_nt(masked(q_dec[ci], h), pair(k_grow[ci], h)), 0.0)
              for h in heads] for ci in chunks]
    else:
        a = [[jnp.zeros((c, c), F32) for _ in heads] for _ in chunks]
        for lv in range(nlev + 1):
            msk = mask_ref[lv]
            for ci in chunks:
                if lv < nlev:
                    w = jnp.exp(-jnp.abs(bc[ci] - rb[ci][(lv + 1) * c:(lv + 2) * c]))
                    qs = (q[ci] * w).astype(BF16)
                    ks = (k[ci] * w).astype(BF16)
                else:
                    qs = q[ci].astype(BF16)
                    ks = k[ci].astype(BF16)
                for h in heads:
                    a[ci][h] = a[ci][h] + _dot_nt(masked(qs, h), pair(ks, h)) * msk

    v_h = [[v[ci][:, h * GLA_DV:(h + 1) * GLA_DV] for h in heads] for ci in chunks]
    o_intra = [[_dot(a[ci][h].astype(BF16), v_h[ci][h]) for h in heads] for ci in chunks]
    upd = [[_dot_tn(v_h[ci][h], pair(k_dec[ci], h)) for h in heads] for ci in chunks]

    st = st_ref[...]
    for ci in chunks:
        st_b = st.astype(BF16)
        for h in heads:
            o = o_intra[ci][h] + _dot_nt(masked(q_dec[ci], h), pair(st_b, h))
            ms = jnp.mean(o * o, axis=-1, keepdims=True)
            on = o * lax.rsqrt(ms + EPS) * nw_ref[...]
            rg = r_ref[0, rows[ci], h * GLA_DV:(h + 1) * GLA_DV].astype(F32)
            o_ref[0, rows[ci], h * GLA_DV:(h + 1) * GLA_DV] = (on * (rg * _sigmoid(rg))).astype(o_ref.dtype)
        st = st_decay[ci] * st + jnp.concatenate(
            [jnp.where(lane_st, upd[ci][2 * pr], upd[ci][2 * pr + 1]) for pr in range(GLA_HEADS // 2)],
            axis=1)
    st_ref[...] = st


def _gla(qk, v, la, r, nw, layer, fast, tg=1024):
    b, t, _ = qk.shape
    sel_np, mask_np = _gla_consts_fast() if fast else _gla_consts()
    sel = jnp.asarray(sel_np, BF16)
    masks = jnp.asarray(mask_np, F32)
    kern = functools.partial(_gla_kernel, n_chunks=tg // GLA_CHUNK, fast=fast)
    seq = lambda w: pl.BlockSpec((1, tg, w), lambda bi, i: (bi, i, 0))
    return pl.pallas_call(
        kern,
        grid=(b, t // tg),
        in_specs=[seq(512), seq(512), seq(GLA_HEADS * GLA_DK), seq(512),
                  _layer_spec(nw, layer), _const_spec(sel.shape), _const_spec(masks.shape)],
        out_specs=seq(512),
        out_shape=jax.ShapeDtypeStruct((b, t, D_MIX), BF16),
        scratch_shapes=[pltpu.VMEM((GLA_DV, GLA_HEADS * GLA_DK), F32)],
        compiler_params=_params("arbitrary", "arbitrary"),
        name="gla_fast" if fast else "gla",
    )(qk, v, la, r, nw, sel, masks)


def _moba_consts():
    blk = MOBA_BLOCK
    tq = _MOBA_QTILES * blk
    c = (2.0 ** (-8.0 * (np.arange(MOBA_HEADS) + 1) / MOBA_HEADS) * LOG2E).astype(np.float32)
    pos = np.arange(blk, dtype=np.float32)
    kx = np.zeros((blk, LANES), np.float32)
    kx[:, 0:3] = pos[:, None]
    pieces = []
    rest = c.copy()
    for _ in range(3):
        piece = rest.astype(jnp.bfloat16).astype(np.float32)
        pieces.append(piece)
        rest = rest - piece
    qx = np.zeros((MOBA_HEADS, LANES, tq), np.float32)
    for j, piece in enumerate(pieces):
        qx[:, j, :] = piece[:, None]
    causal = np.where(pos[:, None] <= pos[None, :], 0.0, NEG).astype(np.float32)
    masks = np.zeros((_MOBA_QTILES + 1, blk, tq), np.float32)
    for t in range(_MOBA_QTILES):
        masks[t + 1, :, t * blk:(t + 1) * blk] = causal
    ucol = c[:, None] * np.arange(tq, dtype=np.float32)[None, :]
    return c, kx, qx, masks, ucol.astype(np.float32)


_MOBA_VROWS = MOBA_DH + 16
_MOBA_UNROLL = 8
_MOBA_QTILES = 1


def _moba_kernel(q_ref, k_ref, v_ref, kx_ref, qx_ref, mask_ref, ucol_ref, o_ref,
                 kmean_ref, vt_ref, acc_ref, m_ref, sel_ref, qa_ref, z_ref, bmax_ref,
                 *, nb, slopes):
    blk = MOBA_BLOCK
    qt = _MOBA_QTILES
    tq = qt * blk
    i0 = pl.program_id(1) * qt

    @pl.when(i0 == 0)
    def _():
        for n in range(nb):
            kb = k_ref[0, n * blk:(n + 1) * blk, :].astype(F32)
            kmean_ref[n:n + 1, :] = jnp.sum(kb, axis=0, keepdims=True) * (1.0 / blk)
            vbt = v_ref[0, n * blk:(n + 1) * blk, :].astype(F32).T
            for h in range(MOBA_HEADS):
                vt_ref[h, 0:MOBA_DH, n * blk:(n + 1) * blk] = (
                    vbt[h * MOBA_DH:(h + 1) * MOBA_DH, :].astype(BF16))
                vt_ref[h, MOBA_DH:_MOBA_VROWS, n * blk:(n + 1) * blk] = jnp.ones(
                    (_MOBA_VROWS - MOBA_DH, blk), BF16)

    lane = lax.broadcasted_iota(jnp.int32, (tq, LANES), 1)
    first_half = lane < MOBA_DH
    rowid = lax.broadcasted_iota(jnp.int32, (nb, tq), 0)
    own_blk = i0 + lax.broadcasted_iota(jnp.int32, (1, tq), 1) // blk

    acc_ref[...] = jnp.zeros_like(acc_ref)
    m_ref[...] = jnp.full_like(m_ref, NEG)

    def scores(h, n, slot, own):
        pr = h // 2
        kp = k_ref[0, pl.ds(pl.multiple_of(n * blk, blk), blk), pr * LANES:(pr + 1) * LANES]
        z = _dot(jnp.concatenate([kp, kx_ref[...]], axis=1), qa_ref[h])
        if not (isinstance(own, int) and own == 0):
            z = z + mask_ref[own]
        z_ref[slot, h] = z
        bmax_ref[slot, h:h + 1, :] = jnp.max(z, axis=0, keepdims=True)

    for h in range(MOBA_HEADS):
        pr = h // 2
        qp = q_ref[0, :, pr * LANES:(pr + 1) * LANES]
        keep = first_half if h % 2 == 0 else jnp.logical_not(first_half)
        qm = jnp.where(keep, qp, jnp.zeros_like(qp))
        qmt = qm.astype(F32).T.astype(BF16)
        qa_ref[h, 0:LANES, :] = qmt
        qa_ref[h, LANES:2 * LANES, :] = qx_ref[h]
        km = kmean_ref[:, pr * LANES:(pr + 1) * LANES]
        km_hi = km.astype(BF16)
        km_lo = (km - km_hi.astype(F32)).astype(BF16)
        g = _dot(km_hi, qmt) + _dot(km_lo, qmt)
        g = jnp.where(rowid < own_blk, g, NEG)
        sel = jnp.where(rowid == own_blk, 1.0, 0.0)
        for r in range(MOBA_TOPK):
            mx = jnp.max(g, axis=0, keepdims=True)
            idx = jnp.min(jnp.where(g == mx, rowid, nb), axis=0, keepdims=True)
            pick = rowid == idx
            sel = jnp.where(jnp.logical_and(pick, own_blk > r), 1.0, sel)
            g = jnp.where(pick, -3.0e38, g)
        sel_ref[h] = sel
        scores(h, 0, 0, (i0 == 0).astype(jnp.int32))

    def accumulate(h, n, slot):
        u = ucol_ref[h:h + 1, :] + (slopes[h] * blk) * (i0 - n).astype(F32)
        picked = sel_ref[h, pl.ds(n, 1), :] > 0.5
        m_old = m_ref[h:h + 1, :]
        m_new = jnp.where(picked, jnp.maximum(m_old, bmax_ref[slot, h:h + 1, :] - u), m_old)
        off = jnp.where(picked, m_new + u, -NEG)
        p = jnp.exp2(z_ref[slot, h] - off)
        alpha = jnp.exp2(m_old - m_new)
        m_ref[h:h + 1, :] = m_new
        vt = vt_ref[h, :, pl.ds(pl.multiple_of(n * blk, blk), blk)]
        acc_ref[h] = alpha * acc_ref[h] + _dot(vt, p.astype(BF16))

    def step(score_args, acc_args):
        for h in range(MOBA_HEADS):
            if score_args is not None:
                scores(h, *score_args)
            if acc_args is not None:
                accumulate(h, *acc_args)

    def regular_steps(k0, count):
        for k in range(count):
            step((k0 + k, (k + 1) % 2, 0), (k0 + k - 1, k % 2))

    def own_steps(first_tile, parity):
        for t in range(first_tile, qt):
            slot = (parity + t - first_tile) % 2
            step((i0 + t, slot, t + 1), (i0 + t - 1, 1 - slot))
        step(None, (i0 + qt - 1, (parity + qt - 1 - first_tile) % 2))

    @pl.when(i0 == 0)
    def _():
        own_steps(1, 1)

    def trip(j, carry):
        regular_steps(_MOBA_UNROLL * j + 1, _MOBA_UNROLL)
        return carry

    n_regular = jnp.maximum(i0 - 1, 0)
    n_trips = n_regular // _MOBA_UNROLL
    lax.fori_loop(0, n_trips, trip, 0)
    k1 = n_trips * _MOBA_UNROLL + 1
    for rem in range(_MOBA_UNROLL):
        if (rem + 1) % math.gcd(_MOBA_UNROLL, qt) != 0:
            continue
        @pl.when(jnp.logical_and(i0 > 0, n_regular % _MOBA_UNROLL == rem))
        def _(rem=rem):
            regular_steps(k1, rem)
            own_steps(0, (rem + 1) % 2)

    out_t = jnp.concatenate(
        [acc_ref[h, 0:MOBA_DH, :] / acc_ref[h, MOBA_DH:MOBA_DH + 1, :] for h in range(MOBA_HEADS)], axis=0)
    o_ref[0] = out_t.T.astype(o_ref.dtype)


def _moba(q, k, v):
    b, t, _ = q.shape
    blk = MOBA_BLOCK
    nb = t // blk
    tq = _MOBA_QTILES * blk
    assert nb % _MOBA_QTILES == 0
    slopes, kx, qx, masks, ucol = _moba_consts()
    kern = functools.partial(_moba_kernel, nb=nb, slopes=tuple(float(s) for s in slopes))
    full = pl.BlockSpec((1, t, D_MIX), lambda bi, i: (bi, 0, 0))
    return pl.pallas_call(
        kern,
        grid=(b, nb // _MOBA_QTILES),
        in_specs=[pl.BlockSpec((1, tq, D_MIX), lambda bi, i: (bi, i, 0)), full, full,
                  _const_spec(kx.shape), _const_spec(qx.shape), _const_spec(masks.shape),
                  _const_spec(ucol.shape)],
        out_specs=pl.BlockSpec((1, tq, D_MIX), lambda bi, i: (bi, i, 0)),
        out_shape=jax.ShapeDtypeStruct((b, t, D_MIX), BF16),
        scratch_shapes=[pltpu.VMEM((nb, D_MIX), F32),
                        pltpu.VMEM((MOBA_HEADS, _MOBA_VROWS, t), BF16),
                        pltpu.VMEM((MOBA_HEADS, _MOBA_VROWS, tq), F32),
                        pltpu.VMEM((MOBA_HEADS, tq), F32),
                        pltpu.VMEM((MOBA_HEADS, nb, tq), F32),
                        pltpu.VMEM((MOBA_HEADS, 2 * LANES, tq), BF16),
                        pltpu.VMEM((2, MOBA_HEADS, blk, tq), F32),
                        pltpu.VMEM((2, MOBA_HEADS, tq), F32)],
        compiler_params=_params("arbitrary", "arbitrary"),
        name="moba",
    )(q, k, v, jnp.asarray(kx, BF16), jnp.asarray(qx, BF16), jnp.asarray(masks), jnp.asarray(ucol))


_POOL_SUB = 128


def _pool_bands():
    r = np.arange(_POOL_SUB)[:, None] + _POOL_SUB
    c = np.arange(2 * _POOL_SUB)[None, :]
    return np.stack([((r - c >= 0) & (r - c < w)).astype(np.float32) for w in POOL_WINDOWS])


def _mix_kernel(x_ref, oa_ref, oc_ref, pu_ref, halo_ref, gates_ref, band_ref, pw_ref, ps_ref,
                wa_ref, wb_ref, wc_ref, wo_ref, nw_ref,
                npre_ref, wg_ref, wu_ref, wd_ref, npost_ref, o_ref, *, seq_len):
    tm = x_ref.shape[0]
    sub = _POOL_SUB
    t0 = (pl.program_id(0) * tm) % seq_len
    halo = halo_ref[...]
    halo = jnp.where(t0 > 0, halo, jnp.zeros_like(halo))
    ext = jnp.concatenate([halo, pu_ref[...]], axis=0)
    n_sub = tm // sub
    sums = [[_dot(band_ref[g], ext[j * sub:(j + 2) * sub, g * POOL_GC:(g + 1) * POOL_GC])
             for g in range(len(POOL_WINDOWS))] for j in range(n_sub)]
    tpos = t0 + lax.broadcasted_iota(jnp.int32, (tm, 1), 0)
    yg = []
    for g, w in enumerate(POOL_WINDOWS):
        s = jnp.concatenate([sums[j][g] for j in range(n_sub)], axis=0)
        cnt = jnp.minimum(tpos + 1, w).astype(F32)
        p = s / cnt - pu_ref[:, g * POOL_GC:(g + 1) * POOL_GC].astype(F32)
        yg.append(_dot(p.astype(BF16), pw_ref[g]))
    y = (jnp.concatenate(yg, axis=1) * ps_ref[...]).astype(BF16)

    ya = _dot(oa_ref[...], wa_ref[...])
    yb = _dot(y, wb_ref[...])
    yc = _dot(oc_ref[...], wc_ref[...])
    d = D_MODEL
    mixed = 0.5 * (jnp.tanh(gates_ref[:, 0:d].astype(F32)) * ya
                   + jnp.tanh(gates_ref[:, d:2 * d].astype(F32)) * yb
                   + jnp.tanh(gates_ref[:, 2 * d:3 * d].astype(F32)) * yc
                   + (ya + yb + yc))
    m2 = _dot(mixed.astype(BF16), wo_ref[...])
    x1 = x_ref[...] + _rms(m2, nw_ref[...])

    sub = _FFN_SUB
    for r0 in range(0, tm, sub):
        xs = x1[r0:r0 + sub, :]
        h = _rms(xs, npre_ref[...]).astype(BF16)
        g = _dot(h, wg_ref[...])
        u = _dot(h, wu_ref[...])
        a = (g * _sigmoid(g) * u).astype(BF16)
        f = _dot(a, wd_ref[...])
        o_ref[r0:r0 + sub, :] = xs + _rms(f, npost_ref[...])


def _mix(x, oa, oc, pu, gates, pool_w, pool_scale, wa, wb, wc, wo, nw, ffn_params, layer, seq_len):
    n = x.shape[0]
    tm = _ROW_TILE
    sub = _POOL_SUB
    bands = jnp.asarray(_pool_bands(), BF16)
    row = lambda w: pl.BlockSpec((tm, w), lambda i: (i, 0))
    halo = pl.BlockSpec((sub, D_MIX), lambda i: (jnp.maximum(i * (tm // sub) - 1, 0), 0))
    params = (pool_w, pool_scale, wa, wb, wc, wo, nw) + tuple(ffn_params)
    return pl.pallas_call(
        functools.partial(_mix_kernel, seq_len=seq_len),
        grid=(n // tm,),
        in_specs=[row(D_MODEL), row(D_MIX), row(D_MIX), row(D_MIX), halo, row(3 * D_MODEL),
                  _const_spec(bands.shape)] + [_layer_spec(p, layer) for p in params],
        out_specs=row(D_MODEL),
        out_shape=jax.ShapeDtypeStruct((n, D_MODEL), F32),
        compiler_params=_params("arbitrary"),
        name="mix",
    )(x, oa, oc, pu, pu, gates, bands, *params)


def _prep_w_in(w):
    sizes = (256, 256, 512, GLA_RANK, 512, 512, 512, 512, 512, 3 * D_MODEL)
    offs = np.concatenate([[0], np.cumsum(sizes)])
    gq, gk, gv, g1, gr, pu, mq, mk, mv, gates = [w[..., offs[j]:offs[j + 1]] for j in range(len(sizes))]
    g1p = jnp.pad(g1, ((0, 0), (0, 0), (0, LANES - GLA_RANK)))
    return jnp.concatenate([gq, gk, gv, gr, pu, mq, mk, mv, gates, g1p], axis=-1).astype(BF16)


def kernel(x, norm_mix_pre, w_in, gla_w_g2, gla_b_g, gla_norm, pool_w, pool_scale, w_branch_a,
           w_branch_b, w_branch_c, w_out, norm_mix_post, norm_ffn_pre, ffn_w_gate, ffn_w_up,
           ffn_w_down, norm_ffn_post):
    b, t, d = x.shape
    n = b * t
    xf = x.reshape(n, d)
    seq = lambda a: a.reshape(b, t, a.shape[-1])
    row = lambda a: a[:, None, :]
    bf = lambda a: a.astype(BF16)
    w_cat = _prep_w_in(w_in)
    wg2 = bf(jnp.pad(gla_w_g2, ((0, 0), (0, LANES - GLA_RANK), (0, 0))))
    mix_params = (bf(pool_w), row(pool_scale), bf(w_branch_a), bf(w_branch_b), bf(w_branch_c),
                  bf(w_out), row(norm_mix_post))
    ffn_params = (row(norm_ffn_pre), bf(ffn_w_gate), bf(ffn_w_up), bf(ffn_w_down), row(norm_ffn_post))
    gla_nw = row(gla_norm)
    for l in range(w_in.shape[0]):
        gqk, gv, gr, pu, mq, mk, mv, gates, la, la_max = _inproj(
            xf, row(norm_mix_pre), w_cat, wg2, row(gla_b_g), l)
        gla_args = (seq(gqk), seq(gv), seq(la), seq(gr), gla_nw)
        bounded = jnp.max(la_max) * (GLA_CHUNK - 1) <= _GLA_FAST_MAX_DECAY
        oa = lax.cond(bounded,
                      functools.partial(_gla, layer=l, fast=True),
                      functools.partial(_gla, layer=l, fast=False), *gla_args)
        oc = _moba(seq(mq), seq(mk), seq(mv))
        xf = _mix(xf, oa.reshape(n, D_MIX), oc.reshape(n, D_MIX), pu, gates, *mix_params,
                  ffn_params, l, t)
    return xf.reshape(b, t, d)
```
